```python
import math
import jax, jax.numpy as jnp
from jax import lax
import numpy as np

D_MODEL = 2048
BATCH = 8
SEQ = 2048
DEPTH = 1
DEC_BATCH = 128
DEC_SEQ = 4
PAST_LEN = 2048
PAGE_SIZE = 128

D_RNN = D_MODEL // 2
RNN_BLOCKS = 8
RNN_BLK = D_RNN // RNN_BLOCKS
RNN_CONV = 4
LRU_C = 8.0
HEAD_DIM = 128
D_ATT = D_MODEL // 2
N_HEADS = D_ATT // HEAD_DIM
N_KV = 2
HPG = N_HEADS // N_KV
KV_W = 2 * N_KV * HEAD_DIM
CMP_BLOCK = 32
CMP_STRIDE = 16
SLC_BLOCK = 64
TOPK = 8
WINDOW = 512
WIN_Q_BLOCK = 128
D_FF = 3 * D_MODEL
FFN_CONV = 3
D_IN = 2 * D_RNN + D_ATT + 3 * KV_W + 3 * N_HEADS + 2 * D_MODEL
RMS_EPS = 1e-6
NEG = -1e30
FORCE = 1e9

kernel_name = 'hawk_nsa_convffn_step'


def rmsnorm(x, g):
    x32 = x.astype(jnp.float32)
    y = x32 * lax.rsqrt(jnp.mean(x32 * x32, axis=-1, keepdims=True) + RMS_EPS)
    return (y * g.astype(jnp.float32)).astype(x.dtype)


def alibi_slopes():
    h = jnp.arange(1, N_HEADS + 1, dtype=jnp.float32)
    return jnp.exp2(-8.0 * h / N_HEADS).reshape(N_KV, HPG)


def masked_softmax(s, mask):
    s = jnp.where(mask, s.astype(jnp.float32), NEG)
    e = jnp.where(mask, jnp.exp(s - jnp.max(s, axis=-1, keepdims=True)), 0.0)
    return e / jnp.maximum(jnp.sum(e, axis=-1, keepdims=True), 1e-30)


def causal_dwconv(x, prev, w, b):
    k = w.shape[0]
    t = x.shape[1]
    xcat = jnp.concatenate([prev.astype(x.dtype), x], axis=1)
    y = b + sum(w[j] * xcat[:, j:j + t] for j in range(k))
    return y, xcat[:, xcat.shape[1] - (k - 1):]


def split_in(z):
    widths = (D_RNN, D_RNN, D_ATT, KV_W, KV_W, KV_W, 3 * N_HEADS, 2 * D_MODEL)
    cuts = [int(c) for c in np.cumsum(widths)[:-1]]
    return jnp.split(z, cuts, axis=-1)


def mixer_inputs(x, norm_g, w_in):
    B, T, _ = x.shape
    h = rmsnorm(x, norm_g)
    xr, gr, q, kvc, kvs, kvw, g_nsa, g_m = split_in(h @ w_in)
    kv_shape = (B, T, 2, N_KV, HEAD_DIM)
    return (xr, gr, q.reshape(B, T, N_KV, HPG, HEAD_DIM), kvc.reshape(kv_shape),
            kvs.reshape(kv_shape), kvw.reshape(kv_shape), g_nsa, g_m)


def rglru(xr, gr, conv_prev, h0, conv_w, conv_b, wa, ba, wx, bx, lam):
    B, T, _ = xr.shape
    xc, new_conv = causal_dwconv(xr, conv_prev, conv_w, conv_b)
    xb = xc.reshape(B, T, RNN_BLOCKS, RNN_BLK)
    r = jax.nn.sigmoid(jnp.einsum('btnc,ncd->btnd', xb, wa).reshape(B, T, D_RNN) + ba)
    i = jax.nn.sigmoid(jnp.einsum('btnc,ncd->btnd', xb, wx).reshape(B, T, D_RNN) + bx)
    log_a = -LRU_C * r.astype(jnp.float32) * jax.nn.softplus(-lam.astype(jnp.float32))
    a = jnp.exp(log_a)
    u = jnp.sqrt(-jnp.expm1(2.0 * log_a)) * (i * xc).astype(jnp.float32)

    def step(h, au):
        h = au[0] * h + au[1]
        return h, h

    h_last, hs = lax.scan(step, h0.astype(jnp.float32), (jnp.swapaxes(a, 0, 1), jnp.swapaxes(u, 0, 1)))
    y = jnp.swapaxes(hs, 0, 1).astype(xr.dtype) * jax.nn.gelu(gr)
    return y, h_last.astype(xr.dtype), new_conv


def chunk_proj(rows, cmp_w):
    B, L = rows.shape[:2]
    ch = rows.reshape(B, L // CMP_STRIDE, CMP_STRIDE, 2, N_KV, HEAD_DIM)
    first = jnp.einsum('bnlcgd,clde->bncge', ch, cmp_w[:, :CMP_STRIDE])
    second = jnp.einsum('bnlcgd,clde->bncge', ch, cmp_w[:, CMP_STRIDE:])
    return first, second


def compress_blocks(first, second, cmp_w, cmp_pos):
    pos_bias = jnp.einsum('cld,clde->ce', cmp_pos, cmp_w)
    return first[:, :-1] + second[:, 1:] + pos_bias[:, None, :]


def compressed_attention(q, qpos, kc, slopes):
    n = kc.shape[1]
    bend = CMP_STRIDE * jnp.arange(n) + (CMP_BLOCK - 1)
    dist = (qpos[:, None] - bend[None, :]).astype(jnp.float32)
    s = jnp.einsum('btgpd,bngd->bgptn', q, kc[:, :, 0]).astype(jnp.float32) / math.sqrt(HEAD_DIM)
    p = masked_softmax(s - slopes[:, :, None, None] * dist, dist >= 0)
    o = jnp.einsum('bgptn,bngd->btgpd', p.astype(q.dtype), kc[:, :, 1])
    return o, p


def cmp_to_slc(n_cmp, n_blk):
    start = CMP_STRIDE * jnp.arange(n_cmp)[:, None]
    b0 = SLC_BLOCK * jnp.arange(n_blk)[None, :]
    return ((start < b0 + SLC_BLOCK) & (start + CMP_BLOCK > b0)).astype(jnp.float32)


def select_blocks(p_cmp, qpos, n_blk):
    imp = jnp.einsum('bgptn,nk->bgtk', p_cmp, cmp_to_slc(p_cmp.shape[-1], n_blk))
    blk = jnp.arange(n_blk)
    cur = (qpos // SLC_BLOCK)[:, None]
    cand = blk[None, :] < cur
    score = jnp.where(cand, jnp.where(blk == 0, FORCE, imp), NEG)
    _, idx = lax.top_k(score, min(TOPK - 1, n_blk))
    return idx, idx < cur


def selected_attention(q, qpos, kv_sel, idx, valid, kv_loc, kpos_loc, slopes):
    B, T = q.shape[:2]
    k1 = idx.shape[-1]
    kpos = idx[..., None] * SLC_BLOCK + jnp.arange(SLC_BLOCK)
    d_sel = (qpos[None, None, :, None, None] - kpos)[:, :, None].astype(jnp.float32)
    s_sel = jnp.einsum('btgpd,bgtksd->bgptks', q, kv_sel[..., 0, :]).astype(jnp.float32) / math.sqrt(HEAD_DIM)
    s_sel = s_sel - slopes[:, :, None, None, None] * d_sel
    m_sel = jnp.broadcast_to(valid[:, :, None, :, :, None], s_sel.shape)
    d_loc = (qpos[:, None] - kpos_loc[None, :]).astype(jnp.float32)
    s_loc = jnp.einsum('btgpd,bsgd->bgpts', q, kv_loc[:, :, 0]).astype(jnp.float32) / math.sqrt(HEAD_DIM)
    s_loc = s_loc - slopes[:, :, None, None] * d_loc
    m_loc = jnp.broadcast_to(d_loc >= 0, s_loc.shape)
    flat = (B, N_KV, HPG, T, k1 * SLC_BLOCK)
    p = masked_softmax(jnp.concatenate([s_sel.reshape(flat), s_loc], -1),
                       jnp.concatenate([m_sel.reshape(flat), m_loc], -1)).astype(q.dtype)
    p_sel = p[..., :k1 * SLC_BLOCK].reshape(s_sel.shape)
    return (jnp.einsum('bgptks,bgtksd->btgpd', p_sel, kv_sel[..., 1, :])
            + jnp.einsum('bgpts,bsgd->btgpd', p[..., k1 * SLC_BLOCK:], kv_loc[:, :, 1]))


def window_attention(q, qpos, kv, kpos, slopes):
    d = qpos[:, None] - kpos[None, :]
    mask = (d >= 0) & (d <= WINDOW) & (kpos[None, :] >= 0)
    s = jnp.einsum('btgpd,bsgd->bgpts', q, kv[:, :, 0]).astype(jnp.float32) / math.sqrt(HEAD_DIM)
    p = masked_softmax(s - slopes[:, :, None, None] * d.astype(jnp.float32), mask)
    return jnp.einsum('bgpts,bsgd->btgpd', p.astype(q.dtype), kv[:, :, 1])


def nsa_prompt(q, kv_cmp, kv_slc, kv_win, cmp_w, cmp_pos, slopes):
    B, T = q.shape[:2]
    qpos = jnp.arange(T)
    first, second = chunk_proj(kv_cmp, cmp_w)
    kc = compress_blocks(first, second, cmp_w, cmp_pos)
    o_cmp, p_cmp = compressed_attention(q, qpos, kc, slopes)
    nb = T // SLC_BLOCK
    idx, valid = select_blocks(p_cmp, qpos, nb)
    k1 = idx.shape[-1]
    kv_blocks = kv_slc.reshape(B, nb, SLC_BLOCK, 2, N_KV, HEAD_DIM)
    kvb = kv_blocks.transpose(0, 4, 1, 2, 3, 5)
    bi = jnp.arange(B)[:, None, None, None]
    gi = jnp.arange(N_KV)[None, :, None, None]

    def one_block(args):
        i, q_i, idx_i, val_i, loc_i = args
        qpos_i = i * SLC_BLOCK + jnp.arange(SLC_BLOCK)
        kv_sel = kvb[bi, gi, idx_i]
        return selected_attention(q_i, qpos_i, kv_sel, idx_i, val_i, loc_i, qpos_i, slopes)

    q_b = jnp.swapaxes(q.reshape(B, nb, SLC_BLOCK, N_KV, HPG, HEAD_DIM), 0, 1)
    idx_b = idx.reshape(B, N_KV, nb, SLC_BLOCK, k1).transpose(2, 0, 1, 3, 4)
    val_b = valid.reshape(B, N_KV, nb, SLC_BLOCK, k1).transpose(2, 0, 1, 3, 4)
    o_slc = lax.map(one_block, (jnp.arange(nb), q_b, idx_b, val_b, jnp.swapaxes(kv_blocks, 0, 1)))
    o_slc = jnp.swapaxes(o_slc, 0, 1).reshape(B, T, N_KV, HPG, HEAD_DIM)
    nq = T // WIN_Q_BLOCK
    kv_pad = jnp.pad(kv_win, ((0, 0), (WINDOW, 0), (0, 0), (0, 0), (0, 0)))
    band = jnp.arange(nq)[:, None] * WIN_Q_BLOCK + jnp.arange(WIN_Q_BLOCK + WINDOW)[None, :]
    o_win = jax.vmap(window_attention, in_axes=(1, 0, 1, 0, None), out_axes=1)(
        q.reshape(B, nq, WIN_Q_BLOCK, N_KV, HPG, HEAD_DIM), qpos.reshape(nq, WIN_Q_BLOCK),
        kv_pad[:, band], band - WINDOW, slopes)
    return o_cmp, o_slc, o_win.reshape(B, T, N_KV, HPG, HEAD_DIM)


def nsa_sample(q, kv_cmp_new, kv_slc_new, kv_win_new, cache_cmp, cache_slc, cache_win, page_table, cmp_w, cmp_pos, slopes):
    DB, Tq = q.shape[:2]
    page = cache_cmp.shape[1]
    past = page_table.shape[1] * page
    qpos = past + jnp.arange(Tq)
    past_cmp = cache_cmp[page_table].reshape(DB, past, 2, N_KV, HEAD_DIM)
    first, second = chunk_proj(past_cmp, cmp_w)
    n_new = Tq // CMP_STRIDE
    if n_new > 0:
        f2, s2 = chunk_proj(kv_cmp_new[:, :n_new * CMP_STRIDE], cmp_w)
        first = jnp.concatenate([first, f2], axis=1)
        second = jnp.concatenate([second, s2], axis=1)
    kc = compress_blocks(first, second, cmp_w, cmp_pos)
    o_cmp, p_cmp = compressed_attention(q, qpos, kc, slopes)
    idx, valid = select_blocks(p_cmp, qpos, past // SLC_BLOCK)
    bpp = page // SLC_BLOCK
    phys = page_table[jnp.arange(DB)[:, None, None, None], idx // bpp]
    rows = (idx % bpp)[..., None] * SLC_BLOCK + jnp.arange(SLC_BLOCK)
    gi = jnp.arange(N_KV)[None, :, None, None, None]
    kv_sel = cache_slc[phys[..., None], rows, :, gi, :]
    o_slc = selected_attention(q, qpos, kv_sel, idx, valid, kv_slc_new, qpos, slopes)
    wb = cache_win.shape[1]
    kv_cat = jnp.concatenate([cache_win, kv_win_new.astype(cache_win.dtype)], axis=1)
    o_win = window_attention(q, qpos, kv_cat, past - wb + jnp.arange(wb + Tq), slopes)
    return o_cmp, o_slc, o_win, kv_cat[:, Tq:]


def merge_out(y_rnn, o_cmp, o_slc, o_win, g_nsa, g_m, w_proj_rnn, w_proj_att, w_out):
    B, T = y_rnn.shape[:2]
    g = jax.nn.sigmoid(g_nsa).reshape(B, T, 3, N_KV, HPG, 1)
    o = (g[:, :, 0] * o_cmp + g[:, :, 1] * o_slc + g[:, :, 2] * o_win).reshape(B, T, D_ATT)
    m = (jax.nn.sigmoid(g_m[..., :D_MODEL]) * (y_rnn @ w_proj_rnn)
         + jax.nn.sigmoid(g_m[..., D_MODEL:]) * (o @ w_proj_att))
    return m @ w_out


def conv_ffn(h, prev, w_gate, w_up, conv_w, conv_b, w_down):
    u, new_conv = causal_dwconv(h @ w_gate, prev, conv_w, conv_b)
    return (jax.nn.gelu(u) * (h @ w_up)) @ w_down, new_conv


def setup_inputs(seed: int = 0) -> dict:
    key = jax.random.key(seed)
    ks = jax.random.split(key, 32)

    def nrm(k, shape, scale):
        return jax.random.normal(k, shape, jnp.float32) * scale

    n_pages = PAST_LEN // PAGE_SIZE
    n_used = DEC_BATCH * n_pages
    n_pool = (5 * n_used) // 4
    page_table = jax.random.permutation(ks[0], n_pool)[:n_used].reshape(DEC_BATCH, n_pages).astype(jnp.int32)
    wbuf = min(WINDOW, PAST_LEN)
    u = jax.random.uniform(ks[1], (DEPTH, D_RNN), jnp.float32, 0.9, 0.999)
    s = u ** (1.0 / LRU_C)
    return {
        'x_prompt': nrm(ks[2], (BATCH, SEQ, D_MODEL), 1.0),
        'x_sample': nrm(ks[3], (DEC_BATCH, DEC_SEQ, D_MODEL), 1.0),
        'cache_kv_cmp': nrm(ks[4], (DEPTH, n_pool, PAGE_SIZE, 2, N_KV, HEAD_DIM), 1.0),
        'cache_kv_slc': nrm(ks[5], (DEPTH, n_pool, PAGE_SIZE, 2, N_KV, HEAD_DIM), 1.0),
        'cache_kv_win': nrm(ks[6], (DEPTH, DEC_BATCH, wbuf, 2, N_KV, HEAD_DIM), 1.0),
        'state_rnn_h': nrm(ks[7], (DEPTH, DEC_BATCH, D_RNN), 0.5),
        'state_rnn_conv': nrm(ks[8], (DEPTH, DEC_BATCH, RNN_CONV - 1, D_RNN), 1.0),
        'state_ffn_conv': nrm(ks[9], (DEPTH, DEC_BATCH, FFN_CONV - 1, D_FF), 1.0),
        'page_table': page_table,
        'norm_mix': 1.0 + nrm(ks[10], (DEPTH, D_MODEL), 0.02),
        'w_in': nrm(ks[11], (DEPTH, D_MODEL, D_IN), D_MODEL ** -0.5),
        'rnn_conv_w': nrm(ks[12], (DEPTH, RNN_CONV, D_RNN), RNN_CONV ** -0.5),
        'rnn_conv_b': nrm(ks[13], (DEPTH, D_RNN), 0.01),
        'rnn_wa': nrm(ks[14], (DEPTH, RNN_BLOCKS, RNN_BLK, RNN_BLK), RNN_BLK ** -0.5),
        'rnn_ba': nrm(ks[15], (DEPTH, D_RNN), 0.01),
        'rnn_wx': nrm(ks[16], (DEPTH, RNN_BLOCKS, RNN_BLK, RNN_BLK), RNN_BLK ** -0.5),
        'rnn_bx': nrm(ks[17], (DEPTH, D_RNN), 0.01),
        'rnn_lambda': jnp.log(s) - jnp.log1p(-s),
        'cmp_w': nrm(ks[18], (DEPTH, 2, CMP_BLOCK, HEAD_DIM, HEAD_DIM), (CMP_BLOCK * HEAD_DIM) ** -0.5),
        'cmp_pos': nrm(ks[19], (DEPTH, 2, CMP_BLOCK, HEAD_DIM), 0.02),
        'w_proj_rnn': nrm(ks[20], (DEPTH, D_RNN, D_MODEL), D_RNN ** -0.5),
        'w_proj_att': nrm(ks[21], (DEPTH, D_ATT, D_MODEL), D_ATT ** -0.5),
        'w_out': nrm(ks[22], (DEPTH, D_MODEL, D_MODEL), D_MODEL ** -0.5),
        'norm_ffn': 1.0 + nrm(ks[23], (DEPTH, D_MODEL), 0.02),
        'ffn_w_gate': nrm(ks[24], (DEPTH, D_MODEL, D_FF), D_MODEL ** -0.5),
        'ffn_w_up': nrm(ks[25], (DEPTH, D_MODEL, D_FF), D_MODEL ** -0.5),
        'ffn_conv_w': nrm(ks[26], (DEPTH, FFN_CONV, D_FF), FFN_CONV ** -0.5),
        'ffn_conv_b': nrm(ks[27], (DEPTH, D_FF), 0.01),
        'ffn_w_down': nrm(ks[28], (DEPTH, D_FF, D_MODEL), D_FF ** -0.5),
        'norm_final': 1.0 + nrm(ks[29], (D_MODEL,), 0.02),
    }


def reference(x_prompt, x_sample, cache_kv_cmp, cache_kv_slc, cache_kv_win, state_rnn_h, state_rnn_conv,
              state_ffn_conv, page_table, norm_mix, w_in, rnn_conv_w, rnn_conv_b, rnn_wa, rnn_ba, rnn_wx, rnn_bx,
              rnn_lambda, cmp_w, cmp_pos, w_proj_rnn, w_proj_att, w_out, norm_ffn, ffn_w_gate, ffn_w_up,
              ffn_conv_w, ffn_conv_b, ffn_w_down, norm_final):
    slopes = alibi_slopes()
    xp, xs = x_prompt, x_sample
    B, T, _ = xp.shape
    DB = xs.shape[0]
    pst = [[] for _ in range(6)]
    sst = [[] for _ in range(6)]
    for l in range(DEPTH):
        rnn_args = (rnn_conv_w[l], rnn_conv_b[l], rnn_wa[l], rnn_ba[l], rnn_wx[l], rnn_bx[l], rnn_lambda[l])
        ffn_args = (ffn_w_gate[l], ffn_w_up[l], ffn_conv_w[l], ffn_conv_b[l], ffn_w_down[l])
        out_args = (w_proj_rnn[l], w_proj_att[l], w_out[l])
        xr, gr, q, kvc, kvs, kvw, g_nsa, g_m = mixer_inputs(xp, norm_mix[l], w_in[l])
        y_rnn, h_last, rconv = rglru(xr, gr, jnp.zeros((B, RNN_CONV - 1, D_RNN), xp.dtype),
                                     jnp.zeros((B, D_RNN), xp.dtype), *rnn_args)
        o_c, o_s, o_w = nsa_prompt(q, kvc, kvs, kvw, cmp_w[l], cmp_pos[l], slopes)
        xp = xp + merge_out(y_rnn, o_c, o_s, o_w, g_nsa, g_m, *out_args)
        f, fconv = conv_ffn(rmsnorm(xp, norm_ffn[l]), jnp.zeros((B, FFN_CONV - 1, D_FF), xp.dtype), *ffn_args)
        xp = xp + f
        for lst, v in zip(pst, (kvc, kvs, kvw[:, T - min(WINDOW, T):], h_last, rconv, fconv)):
            lst.append(v)
        xr, gr, q, kvc, kvs, kvw, g_nsa, g_m = mixer_inputs(xs, norm_mix[l], w_in[l])
        y_rnn, h_last, rconv = rglru(xr, gr, state_rnn_conv[l], state_rnn_h[l], *rnn_args)
        o_c, o_s, o_w, win_buf = nsa_sample(q, kvc, kvs, kvw, cache_kv_cmp[l], cache_kv_slc[l], cache_kv_win[l],
                                            page_table, cmp_w[l], cmp_pos[l], slopes)
        xs = xs + merge_out(y_rnn, o_c, o_s, o_w, g_nsa, g_m, *out_args)
        f, fconv = conv_ffn(rmsnorm(xs, norm_ffn[l]), state_ffn_conv[l], *ffn_args)
        xs = xs + f
        for lst, v in zip(sst, (kvc, kvs, win_buf, h_last, rconv, fconv)):
            lst.append(v)
    y_prompt = rmsnorm(xp, norm_final)
    y_sample = rmsnorm(xs, norm_final)
    p_kv_cmp, p_kv_slc, p_kv_win, p_rnn_h, p_rnn_conv, p_ffn_conv = [jnp.stack(v) for v in pst]
    s_kv_cmp, s_kv_slc, s_kv_win, s_rnn_h, s_rnn_conv, s_ffn_conv = [jnp.stack(v) for v in sst]
    return (y_prompt, y_sample, p_kv_cmp, p_kv_slc, p_kv_win, p_rnn_h, p_rnn_conv, p_ffn_conv,
            s_kv_cmp, s_kv_slc, s_kv_win, s_rnn_h, s_rnn_conv, s_ffn_conv)
```

```python
import functools
import math

import jax
import jax.numpy as jnp
from jax import lax
from jax.experimental import pallas as pl
from jax.experimental.pallas import tpu as pltpu

D_MODEL = 2048
D_RNN = D_MODEL // 2
RNN_BLOCKS = 8
RNN_BLK = D_RNN // RNN_BLOCKS
RNN_CONV = 4
LRU_C = 8.0
HEAD_DIM = 128
D_ATT = D_MODEL // 2
N_HEADS = D_ATT // HEAD_DIM
N_KV = 2
HPG = N_HEADS // N_KV
KV_W = 2 * N_KV * HEAD_DIM
CMP_BLOCK = 32
CMP_STRIDE = 16
SLC_BLOCK = 64
TOPK = 8
WINDOW = 512
WIN_Q_BLOCK = 128
D_FF = 3 * D_MODEL
FFN_CONV = 3
RMS_EPS = 1e-6
NEG = -1e30

COL_XR = 0
COL_GR = COL_XR + D_RNN
COL_GM = COL_GR + D_RNN
COL_Q = COL_GM + 2 * D_MODEL
COL_KVC = COL_Q + D_ATT
COL_KVS = COL_KVC + KV_W
COL_KVW = COL_KVS + KV_W
COL_GN = COL_KVW + KV_W
GN_PAD = KV_W
D_Z = COL_GN + GN_PAD

VMEM_LIMIT_BYTES = 56 * 1024 * 1024
SUBLANES = 8
LANES = 128

WIN_KEYS = WINDOW + WIN_Q_BLOCK
SLC_CHUNK = 512

f32 = jnp.float32
bf16 = jnp.bfloat16


def _round_up(x, m):
    return (x + m - 1) // m * m


def _params(*sem):
    return pltpu.CompilerParams(dimension_semantics=sem, vmem_limit_bytes=VMEM_LIMIT_BYTES)


def _rms(x, g):
    return (x * lax.rsqrt(jnp.mean(x * x, axis=-1, keepdims=True) + RMS_EPS)) * g


def _gelu(x):
    return x * (0.5 * (1.0 + jnp.tanh(math.sqrt(2.0 / math.pi) * (x + 0.044715 * (x * x * x)))))


def _nt(a, b):
    return lax.dot_general(a, b, (((1,), (1,)), ((), ())), preferred_element_type=f32)


def _inproj_kernel(x_ref, g_ref, w_ref, o_ref, h_ref):
    @pl.when(pl.program_id(1) == 0)
    def _():
        h_ref[...] = _rms(x_ref[...], g_ref[...]).astype(bf16)

    o_ref[...] = jnp.dot(h_ref[...], w_ref[...], preferred_element_type=f32)


def _inproj(x2d, g, w_cat):
    rows = x2d.shape[0]
    tm = min(512, rows)
    tn = 1024
    return pl.pallas_call(
        _inproj_kernel,
        grid=(rows // tm, D_Z // tn),
        in_specs=[
            pl.BlockSpec((tm, D_MODEL), lambda i, j: (i, 0)),
            pl.BlockSpec((1, D_MODEL), lambda i, j: (0, 0)),
            pl.BlockSpec((D_MODEL, tn), lambda i, j: (0, j)),
        ],
        out_specs=pl.BlockSpec((tm, tn), lambda i, j: (i, j)),
        out_shape=jax.ShapeDtypeStruct((rows, D_Z), f32),
        scratch_shapes=[pltpu.VMEM((tm, D_MODEL), bf16)],
        compiler_params=_params("parallel", "arbitrary"),
        name="inproj",
    )(x2d, g, w_cat)


def _rglru_kernel(xr_ref, gr_ref, prev_ref, h0_ref, cw_ref, cb_ref, wg_ref, ba_ref, bx_ref, lam_ref,
                  y_ref, hl_ref, nc_ref, xcat, hc, sa, su, *, S, Tt):
    rows = Tt * S
    k1 = RNN_CONV - 1
    pad = _round_up(k1 * S, SUBLANES)
    ipad = max((Tt // 2) * S, SUBLANES)

    @pl.when(pl.program_id(1) == 0)
    def _():
        xcat[pad - k1 * S:pad, :] = prev_ref[...]
        hc[...] = h0_ref[...]
        for k in range(2):
            sa[k, 0:ipad, :] = jnp.ones((ipad, D_RNN), f32)
            su[k, 0:ipad, :] = jnp.zeros((ipad, D_RNN), f32)

    xcat[pad:pad + rows, :] = xr_ref[...]
    xc = cb_ref[...] + cw_ref[k1:k1 + 1, :] * xr_ref[...]
    for j in range(k1):
        off = pad - (k1 - j) * S
        xc = xc + cw_ref[j:j + 1, :] * xcat[off:off + rows, :]
    tail = xcat[pad + rows - k1 * S:pad + rows, :]
    xcat[pad - k1 * S:pad, :] = tail
    nc_ref[...] = tail

    lam = lam_ref[...]
    softplus_neg_lam = jnp.maximum(-lam, 0.0) + jnp.log1p(jnp.exp(-jnp.abs(lam)))
    xcb = xc.astype(bf16)
    for n in range(RNN_BLOCKS):
        sl = slice(n * RNN_BLK, (n + 1) * RNN_BLK)
        gates = jnp.dot(xcb[:, sl], wg_ref[n], preferred_element_type=f32)
        r = jax.nn.sigmoid(gates[:, :RNN_BLK] + ba_ref[:, sl])
        i = jax.nn.sigmoid(gates[:, RNN_BLK:] + bx_ref[:, sl])
        log_a = (-LRU_C) * r * softplus_neg_lam[:, sl]
        a = jnp.exp(log_a)
        one_minus_a2 = -jnp.tanh(log_a) * (a * a + 1.0)
        sa[0, ipad:ipad + rows, sl] = a
        su[0, ipad:ipad + rows, sl] = jnp.sqrt(one_minus_a2) * (i * xc[:, sl])

    n_rounds = Tt.bit_length() - 1
    for rnd in range(n_rounds):
        d = (1 << rnd) * S
        src, dst = rnd % 2, 1 - rnd % 2
        a = sa[src, ipad:ipad + rows, :]
        su[dst, ipad:ipad + rows, :] = a * su[src, ipad - d:ipad - d + rows, :] + su[src, ipad:ipad + rows, :]
        sa[dst, ipad:ipad + rows, :] = a * sa[src, ipad - d:ipad - d + rows, :]
    fin = n_rounds % 2
    a_cum = sa[fin, ipad:ipad + rows, :]
    u_cum = su[fin, ipad:ipad + rows, :]
    if S == 1:
        h = a_cum * hc[...] + u_cum
    else:
        h = (a_cum.reshape(Tt, S, D_RNN) * hc[...][None] + u_cum.reshape(Tt, S, D_RNN)).reshape(rows, D_RNN)
    y_ref[...] = h * _gelu(gr_ref[...])
    su[fin, ipad:ipad + rows, :] = h
    last = su[fin, ipad + rows - S:ipad + rows, :]
    hc[...] = last
    hl_ref[...] = last


def _rglru(z3, prev, h0, cw, cb, wg, ba, bx, lam, *, S, Tt):
    nb, rper, _ = z3.shape
    rows = Tt * S
    assert Tt & (Tt - 1) == 0 and rper % rows == 0 and Tt >= RNN_CONV - 1
    k1 = RNN_CONV - 1
    pad = _round_up(k1 * S, SUBLANES)
    ipad = max((Tt // 2) * S, SUBLANES)
    vec = lambda: pl.BlockSpec((1, D_RNN), lambda b, t: (0, 0))
    return pl.pallas_call(
        functools.partial(_rglru_kernel, S=S, Tt=Tt),
        grid=(nb, rper // rows),
        in_specs=[
            pl.BlockSpec((None, rows, D_RNN), lambda b, t: (b, t, COL_XR // D_RNN)),
            pl.BlockSpec((None, rows, D_RNN), lambda b, t: (b, t, COL_GR // D_RNN)),
            pl.BlockSpec((None, k1 * S, D_RNN), lambda b, t: (b, 0, 0)),
            pl.BlockSpec((None, S, D_RNN), lambda b, t: (b, 0, 0)),
            pl.BlockSpec((RNN_CONV, D_RNN), lambda b, t: (0, 0)),
            vec(),
            pl.BlockSpec((RNN_BLOCKS, RNN_BLK, 2 * RNN_BLK), lambda b, t: (0, 0, 0)),
            vec(), vec(), vec(),
        ],
        out_specs=[
            pl.BlockSpec((None, rows, D_RNN), lambda b, t: (b, t, 0)),
            pl.BlockSpec((None, S, D_RNN), lambda b, t: (b, 0, 0)),
            pl.BlockSpec((None, k1 * S, D_RNN), lambda b, t: (b, 0, 0)),
        ],
        out_shape=[
            jax.ShapeDtypeStruct((nb, rper, D_RNN), f32),
            jax.ShapeDtypeStruct((nb, S, D_RNN), f32),
            jax.ShapeDtypeStruct((nb, k1 * S, D_RNN), f32),
        ],
        scratch_shapes=[
            pltpu.VMEM((pad + rows, D_RNN), f32),
            pltpu.VMEM((S, D_RNN), f32),
            pltpu.VMEM((2, ipad + rows, D_RNN), f32),
            pltpu.VMEM((2, ipad + rows, D_RNN), f32),
        ],
        compiler_params=_params("parallel", "arbitrary"),
        name="rglru",
    )(z3, z3, prev, h0, cw, cb, wg, ba, bx, lam)


def _kc_kernel(*refs, n_pieces):
    refs = refs[len(refs) - (n_pieces + 4):]
    x_refs = refs[:n_pieces]
    wc_ref, pw_ref, pos_ref, kc_ref = refs[n_pieces:]
    nc = sum(r.shape[0] for r in x_refs)
    for c in range(2):
        groups = []
        for g in range(N_KV):
            col = (c * N_KV + g) * HEAD_DIM
            pieces = [jnp.concatenate([r[:, l * KV_W + col:l * KV_W + col + HEAD_DIM] for l in range(CMP_STRIDE)],
                                      axis=1) for r in x_refs]
            groups.append(pieces[0] if n_pieces == 1 else jnp.concatenate(pieces, axis=0))
        xs = jnp.concatenate(groups, axis=0).astype(bf16)
        fs = jnp.dot(xs, wc_ref[c], preferred_element_type=f32)
        pos = jnp.broadcast_to(pos_ref[c], (SUBLANES, CMP_BLOCK * HEAD_DIM)).astype(bf16)
        pos_bias = jnp.dot(pos, pw_ref[c], preferred_element_type=f32)[0:1, :]
        for g in range(N_KV):
            first = fs[g * nc:(g + 1) * nc, :HEAD_DIM]
            second = fs[g * nc:(g + 1) * nc, HEAD_DIM:]
            kc_ref[c * N_KV + g] = first + pltpu.roll(second, nc - 1, 0) + pos_bias


def _kc_call(x_arrays, x_specs, nseq, nc, wc, pw, pos, page_table=None):
    n_pieces = len(x_arrays)
    nsp = 0 if page_table is None else 1
    const = lambda shape: pl.BlockSpec(shape, lambda b, *_: (0,) * len(shape))
    grid_spec = pltpu.PrefetchScalarGridSpec(
        num_scalar_prefetch=nsp,
        grid=(nseq,),
        in_specs=list(x_specs) + [
            const((2, CMP_STRIDE * HEAD_DIM, 2 * HEAD_DIM)),
            const((2, CMP_BLOCK * HEAD_DIM, HEAD_DIM)),
            const((2, 1, CMP_BLOCK * HEAD_DIM)),
        ],
        out_specs=pl.BlockSpec((None, 2 * N_KV, nc, HEAD_DIM), lambda b, *_: (b, 0, 0, 0)),
    )
    args = ([] if page_table is None else [page_table]) + list(x_arrays) + [wc, pw, pos]
    return pl.pallas_call(
        functools.partial(_kc_kernel, n_pieces=n_pieces),
        grid_spec=grid_spec,
        out_shape=jax.ShapeDtypeStruct((nseq, 2 * N_KV, nc, HEAD_DIM), f32),
        compiler_params=_params("parallel"),
        name="kc_sample" if nsp else "kc_prompt",
    )(*args)


def _softmax_rows(s, mask):
    s = jnp.where(mask, s, NEG)
    e = jnp.where(mask, jnp.exp(s - jnp.max(s, axis=-1, keepdims=True)), 0.0)
    return e / jnp.maximum(jnp.sum(e, axis=-1, keepdims=True), 1e-30)


def _select_blocks(psum, qpos, tb):
    lanes_per_blk = SLC_BLOCK // CMP_STRIDE
    n_lane = psum.shape[1]
    imp = psum + pltpu.roll(psum, 1, 1)
    for back in range(1, lanes_per_blk):
        imp = imp + pltpu.roll(psum, n_lane - back, 1)
    lane = lax.broadcasted_iota(jnp.int32, (tb, n_lane), 1)
    lane_f = lane.astype(f32)
    blk = lane >> (lanes_per_blk.bit_length() - 1)
    cur = qpos >> (SLC_BLOCK.bit_length() - 1)
    cand = ((lane & (lanes_per_blk - 1)) == 0) & (blk >= 1) & (blk < cur)
    score = jnp.where(cand, imp, NEG)
    sel = (lane == 0) & (cur > 0)
    for _ in range(TOPK - 2):
        best = jnp.max(score, axis=-1, keepdims=True)
        first = jnp.min(jnp.where(score == best, lane_f, float(n_lane)), axis=-1, keepdims=True)
        pick = (lane_f == first) & (best > 0.5 * NEG)
        sel = sel | pick
        score = jnp.where(pick, NEG, score)
    return jnp.where(sel, 1.0, 0.0).astype(bf16)


def _attn_core(q, qpos, sg, kc_ref, ks_ref, kw_ref, slc_state, *, tb, slc_chunks, win_row0, win_pos0):
    scale = 1.0 / math.sqrt(HEAD_DIM)
    n_cmp = kc_ref.shape[1]
    slc_shift = SLC_BLOCK.bit_length() - 1
    cur = qpos >> slc_shift
    lane_c = lax.broadcasted_iota(jnp.int32, (tb, n_cmp), 1)
    dist_c = (qpos - (CMP_STRIDE * lane_c + (CMP_BLOCK - 1))).astype(f32)
    mask_c = dist_c >= 0
    outs = []
    for g in range(N_KV):
        slopes = [2.0 ** (-8.0 * (g * HPG + p + 1) / N_HEADS) for p in range(HPG)]
        hs = lambda x, p: x[p * tb:(p + 1) * tb]
        qg = jnp.concatenate([q[:, (g * HPG + p) * HEAD_DIM:(g * HPG + p + 1) * HEAD_DIM] for p in range(HPG)],
                             axis=0).astype(bf16)

        s = _nt(qg, kc_ref[g].astype(bf16)) * scale
        ps = [_softmax_rows(hs(s, p) - slopes[p] * dist_c, mask_c) for p in range(HPG)]
        o_cmp = jnp.dot(jnp.concatenate(ps, axis=0).astype(bf16), kc_ref[N_KV + g].astype(bf16),
                        preferred_element_type=f32)
        selm = _select_blocks(ps[0] + ps[1] + ps[2] + ps[3], qpos, tb)

        def slc_chunk(k0, width, st):
            m, l, acc = st
            kk = ks_ref[pl.ds(k0, width), g * HEAD_DIM:(g + 1) * HEAD_DIM]
            vv = ks_ref[pl.ds(k0, width), (N_KV + g) * HEAD_DIM:(N_KV + g + 1) * HEAD_DIM]
            s = _nt(qg, kk) * scale
            kpos = k0 + lax.broadcasted_iota(jnp.int32, (1, width), 1)
            kblk = kpos >> slc_shift
            hot =lax.broadcasted_iota(jnp.int32, (n_cmp, width), 0) == kblk * (SLC_BLOCK // CMP_STRIDE)
            chosen = jnp.dot(selm, jnp.where(hot, 1.0, 0.0).astype(bf16), preferred_element_type=f32)
            d = qpos - kpos
            mask = (chosen > 0.5) | ((kblk == cur) & (d >= 0))
            df = d.astype(f32)
            ms, ls, es, alphas = [], [], [], []
            for p in range(HPG):
                sp = jnp.where(mask, hs(s, p) - slopes[p] * df, NEG)
                m_new = jnp.maximum(hs(m, p), jnp.max(sp, axis=-1, keepdims=True))
                alpha = jnp.exp(hs(m, p) - m_new)
                e = jnp.where(mask, jnp.exp(sp - m_new), 0.0)
                ms.append(m_new)
                ls.append(alpha * hs(l, p) + jnp.sum(e, axis=-1, keepdims=True))
                es.append(e)
                alphas.append(alpha)
            pv = jnp.dot(jnp.concatenate(es, axis=0).astype(bf16), vv, preferred_element_type=f32)
            return (jnp.concatenate(ms, axis=0), jnp.concatenate(ls, axis=0),
                    jnp.concatenate(alphas, axis=0) * acc + pv)

        m_s, l_s, acc_s = slc_state
        m_s[...] = jnp.full((HPG * tb, 1), NEG, f32)
        l_s[...] = jnp.zeros((HPG * tb, 1), f32)
        acc_s[...] = jnp.zeros((HPG * tb, HEAD_DIM), f32)

        for k0, width in slc_chunks:
            m_s[...], l_s[...], acc_s[...] = slc_chunk(k0, width, (m_s[...], l_s[...], acc_s[...]))
        o_slc = acc_s[...] / jnp.maximum(l_s[...], 1e-30)

        kk = kw_ref[pl.ds(win_row0, WIN_KEYS), g * HEAD_DIM:(g + 1) * HEAD_DIM]
        vv = kw_ref[pl.ds(win_row0, WIN_KEYS), (N_KV + g) * HEAD_DIM:(N_KV + g + 1) * HEAD_DIM]
        s = _nt(qg, kk) * scale
        kpos = win_pos0 + lax.broadcasted_iota(jnp.int32, (1, WIN_KEYS), 1)
        d = qpos - kpos
        mask_w = (d >= 0) & (d <= WINDOW) & (kpos >= 0)
        df = d.astype(f32)
        ps = [_softmax_rows(hs(s, p) - slopes[p] * df, mask_w) for p in range(HPG)]
        o_win = jnp.dot(jnp.concatenate(ps, axis=0).astype(bf16), vv, preferred_element_type=f32)

        for p in range(HPG):
            h = g * HPG + p
            outs.append(sg[:, h:h + 1] * hs(o_cmp, p) + sg[:, N_HEADS + h:N_HEADS + h + 1] * hs(o_slc, p)
                        + sg[:, 2 * N_HEADS + h:2 * N_HEADS + h + 1] * hs(o_win, p))
    return jnp.concatenate(outs, axis=1)


def _slc_state_scratch(tb):
    return [pltpu.VMEM((HPG * tb, 1), f32), pltpu.VMEM((HPG * tb, 1), f32), pltpu.VMEM((HPG * tb, HEAD_DIM), f32)]


def _attn_prompt_kernel(q_ref, kvs_ref, kvw_ref, gn_ref, kc_ref, o_ref, ks_bf, kw_bf, *slc_state, T, sb):
    tb = WIN_Q_BLOCK
    i = sb * (SLC_CHUNK // tb) + pl.program_id(1)

    @pl.when(pl.program_id(1) == 0)
    def _():
        ks_bf[...] = kvs_ref[...].astype(bf16)
        kw_bf[...] = kvw_ref[...].astype(bf16)

    qpos = i * tb + lax.broadcasted_iota(jnp.int32, (tb, 1), 0)
    win0 = pl.multiple_of(jnp.clip(i * tb - WINDOW, 0, T - WIN_KEYS), LANES)
    o_ref[...] = _attn_core(
        q_ref[...], qpos, jax.nn.sigmoid(gn_ref[:, 0:LANES]), kc_ref, ks_bf, kw_bf, slc_state, tb=tb,
        slc_chunks=[(k0, SLC_CHUNK) for k0 in range(0, (sb + 1) * SLC_CHUNK, SLC_CHUNK)],
        win_row0=win0, win_pos0=win0)


def _attn_prompt(z3, kc, sb):
    B, T, _ = z3.shape
    assert T % SLC_CHUNK == 0 and T >= WIN_KEYS
    tb = WIN_Q_BLOCK
    nq = SLC_CHUNK // tb
    return pl.pallas_call(
        functools.partial(_attn_prompt_kernel, T=T, sb=sb),
        grid=(B, nq),
        in_specs=[
            pl.BlockSpec((None, tb, D_ATT), lambda b, i: (b, sb * nq + i, COL_Q // D_ATT)),
            pl.BlockSpec((None, T, KV_W), lambda b, i: (b, 0, COL_KVS // KV_W)),
            pl.BlockSpec((None, T, KV_W), lambda b, i: (b, 0, COL_KVW // KV_W)),
            pl.BlockSpec((None, tb, GN_PAD), lambda b, i: (b, sb * nq + i, COL_GN // GN_PAD)),
            pl.BlockSpec((None, 2 * N_KV, T // CMP_STRIDE, HEAD_DIM), lambda b, i: (b, 0, 0, 0)),
        ],
        out_specs=pl.BlockSpec((None, tb, D_ATT), lambda b, i: (b, i, 0)),
        out_shape=jax.ShapeDtypeStruct((B, SLC_CHUNK, D_ATT), f32),
        scratch_shapes=[pltpu.VMEM((T, KV_W), bf16), pltpu.VMEM((T, KV_W), bf16)] + _slc_state_scratch(tb),
        compiler_params=_params("parallel", "arbitrary"),
        name=f"attn_prompt_{sb}",
    )(z3, z3, z3, z3, kc)


def _attn_sample_kernel(*refs, n_pages, page, tq, tb):
    zq_ref, kc_ref, win_ref = refs[1:4]
    page_refs = refs[4:4 + n_pages]
    o_ref, skw_ref, ks_bf, kw_bf = refs[4 + n_pages:8 + n_pages]
    slc_state = refs[8 + n_pages:]
    past = n_pages * page
    wbuf = win_ref.shape[0]
    q0 = 0
    c_kvs = q0 + D_ATT + KV_W
    c_kvw = c_kvs + KV_W
    c_gn = c_kvw + KV_W
    for j in range(n_pages):
        ks_bf[j * page:(j + 1) * page, :] = page_refs[j][...].astype(bf16)
    zero_rows = jnp.zeros((LANES - tb, KV_W), f32)
    ks_bf[past:past + LANES, :] = jnp.concatenate([zq_ref[:, c_kvs:c_kvs + KV_W], zero_rows], axis=0).astype(bf16)
    kw_bf[0:wbuf, :] = win_ref[...].astype(bf16)
    kw_bf[wbuf:wbuf + LANES, :] = jnp.concatenate([zq_ref[:, c_kvw:c_kvw + KV_W], zero_rows], axis=0).astype(bf16)
    skw_ref[0:wbuf - tq, :] = win_ref[tq:wbuf, :]
    skw_ref[wbuf - tq:wbuf, :] = zq_ref[0:tq, c_kvw:c_kvw + KV_W]

    qpos = past + lax.broadcasted_iota(jnp.int32, (tb, 1), 0)
    chunks = [(k0, SLC_CHUNK) for k0 in range(0, past, SLC_CHUNK)] + [(past, LANES)]
    o_ref[...] = _attn_core(
        zq_ref[:, q0:q0 + D_ATT], qpos, jax.nn.sigmoid(zq_ref[:, c_gn:c_gn + LANES]), kc_ref, ks_bf, kw_bf,
        slc_state, tb=tb, slc_chunks=chunks, win_row0=0, win_pos0=past - wbuf)


def _attn_sample(zq, kc, cache_win, cache_slc, page_table, *, tq):
    DB, tb, zw = zq.shape
    n_pages = page_table.shape[1]
    page = cache_slc.shape[1]
    past = n_pages * page
    wbuf = cache_win.shape[1]
    assert past % SLC_CHUNK == 0 and wbuf + LANES == WIN_KEYS and tq <= tb and tq < CMP_STRIDE
    page_specs = [pl.BlockSpec((None, page, KV_W), functools.partial(lambda b, pt, j: (pt[b, j], 0, 0), j=j))
                  for j in range(n_pages)]
    grid_spec = pltpu.PrefetchScalarGridSpec(
        num_scalar_prefetch=1,
        grid=(DB,),
        in_specs=[
            pl.BlockSpec((None, tb, zw), lambda b, pt: (b, 0, 0)),
            pl.BlockSpec((None, 2 * N_KV, past // CMP_STRIDE, HEAD_DIM), lambda b, pt: (b, 0, 0, 0)),
            pl.BlockSpec((None, wbuf, KV_W), lambda b, pt: (b, 0, 0)),
        ] + page_specs,
        out_specs=[
            pl.BlockSpec((None, tb, D_ATT), lambda b, pt: (b, 0, 0)),
            pl.BlockSpec((None, wbuf, KV_W), lambda b, pt: (b, 0, 0)),
        ],
        scratch_shapes=[pltpu.VMEM((past + LANES, KV_W), bf16), pltpu.VMEM((wbuf + LANES, KV_W), bf16)]
        + _slc_state_scratch(tb),
    )
    return pl.pallas_call(
        functools.partial(_attn_sample_kernel, n_pages=n_pages, page=page, tq=tq, tb=tb),
        grid_spec=grid_spec,
        out_shape=[jax.ShapeDtypeStruct((DB, tb, D_ATT), f32), jax.ShapeDtypeStruct((DB, wbuf, KV_W), f32)],
        compiler_params=_params("parallel"),
        name="attn_sample",
    )(page_table, zq, kc, cache_win, *([cache_slc] * n_pages))


def _merge_kernel(x_ref, yr_ref, o_ref, gm1_ref, gm2_ref, wr_ref, wa_ref, wo_ref, out_ref):
    m = (jax.nn.sigmoid(gm1_ref[...]) * jnp.dot(yr_ref[...].astype(bf16), wr_ref[...], preferred_element_type=f32)
         + jax.nn.sigmoid(gm2_ref[...]) * jnp.dot(o_ref[...].astype(bf16), wa_ref[...], preferred_element_type=f32))
    out_ref[...] = x_ref[...] + jnp.dot(m.astype(bf16), wo_ref[...], preferred_element_type=f32)


def _merge(x2d, yr, o, z2d, wr, wa, wo):
    rows = x2d.shape[0]
    tm = min(256, rows)
    const = lambda shape: pl.BlockSpec(shape, lambda i: (0, 0), pipeline_mode=pl.Buffered(1))
    return pl.pallas_call(
        _merge_kernel,
        grid=(rows // tm,),
        in_specs=[
            pl.BlockSpec((tm, D_MODEL), lambda i: (i, 0)),
            pl.BlockSpec((tm, D_RNN), lambda i: (i, 0)),
            pl.BlockSpec((tm, D_ATT), lambda i: (i, 0)),
            pl.BlockSpec((tm, D_MODEL), lambda i: (i, COL_GM // D_MODEL)),
            pl.BlockSpec((tm, D_MODEL), lambda i: (i, COL_GM // D_MODEL + 1)),
            const((D_RNN, D_MODEL)), const((D_ATT, D_MODEL)), const((D_MODEL, D_MODEL)),
        ],
        out_specs=pl.BlockSpec((tm, D_MODEL), lambda i: (i, 0)),
        out_shape=jax.ShapeDtypeStruct((rows, D_MODEL), f32),
        compiler_params=_params("parallel"),
        name="merge",
    )(x2d, yr, o, z2d, z2d, wr, wa, wo)


def _ffn_kernel(x_ref, prev_ref, gf_ref, wg_ref, wu_ref, wd_ref, cw_ref, cb_ref, gl_ref,
                y_ref, fc_ref, h_ref, acc_ref, gcat, carry, *, S, rows, final_norm):
    ti, fi = pl.program_id(1), pl.program_id(2)
    k1 = FFN_CONV - 1
    pad = _round_up(k1 * S, SUBLANES)

    @pl.when(fi == 0)
    def _():
        h_ref[...] = _rms(x_ref[...], gf_ref[...]).astype(bf16)
        acc_ref[...] = jnp.zeros_like(acc_ref)

    @pl.when(ti == 0)
    def _():
        carry[fi, pad - k1 * S:pad, :] = prev_ref[...]

    h = h_ref[...]
    gate = jnp.dot(h, wg_ref[...], preferred_element_type=f32)
    gcat[pad - k1 * S:pad, :] = carry[fi, pad - k1 * S:pad, :]
    gcat[pad:pad + rows, :] = gate
    u = cb_ref[...] + cw_ref[k1:k1 + 1, :] * gate
    for j in range(k1):
        off = pad - (k1 - j) * S
        u = u + cw_ref[j:j + 1, :] * gcat[off:off + rows, :]
    tail = gcat[pad + rows - k1 * S:pad + rows, :]
    carry[fi, pad - k1 * S:pad, :] = tail
    fc_ref[...] = tail
    up = jnp.dot(h, wu_ref[...], preferred_element_type=f32)
    acc_ref[...] += jnp.dot((_gelu(u) * up).astype(bf16), wd_ref[...], preferred_element_type=f32)

    @pl.when(fi == pl.num_programs(2) - 1)
    def _():
        out = x_ref[...] + acc_ref[...]
        y_ref[...] = _rms(out, gl_ref[...]) if final_norm else out


def _ffn(x3, prev, g_ffn, wg, wu, wd, cw, cb, g_fin, *, S, Tt, final_norm):
    nb, rper, _ = x3.shape
    rows = Tt * S
    tf = 512
    nf = D_FF // tf
    k1 = FFN_CONV - 1
    pad = _round_up(k1 * S, SUBLANES)
    assert rper % rows == 0 and Tt >= k1
    return pl.pallas_call(
        functools.partial(_ffn_kernel, S=S, rows=rows, final_norm=final_norm),
        grid=(nb, rper // rows, nf),
        in_specs=[
            pl.BlockSpec((None, rows, D_MODEL), lambda b, t, f: (b, t, 0)),
            pl.BlockSpec((None, k1 * S, tf), lambda b, t, f: (b, 0, f)),
            pl.BlockSpec((1, D_MODEL), lambda b, t, f: (0, 0)),
            pl.BlockSpec((D_MODEL, tf), lambda b, t, f: (0, f)),
            pl.BlockSpec((D_MODEL, tf), lambda b, t, f: (0, f)),
            pl.BlockSpec((tf, D_MODEL), lambda b, t, f: (f, 0)),
            pl.BlockSpec((FFN_CONV, tf), lambda b, t, f: (0, f)),
            pl.BlockSpec((1, tf), lambda b, t, f: (0, f)),
            pl.BlockSpec((1, D_MODEL), lambda b, t, f: (0, 0)),
        ],
        out_specs=[
            pl.BlockSpec((None, rows, D_MODEL), lambda b, t, f: (b, t, 0)),
            pl.BlockSpec((None, None, k1 * S, tf), lambda b, t, f: (b, t, 0, f)),
        ],
        out_shape=[
            jax.ShapeDtypeStruct((nb, rper, D_MODEL), f32),
            jax.ShapeDtypeStruct((nb, rper // rows, k1 * S, D_FF), f32),
        ],
        scratch_shapes=[
            pltpu.VMEM((rows, D_MODEL), bf16),
            pltpu.VMEM((rows, D_MODEL), f32),
            pltpu.VMEM((pad + rows, tf), f32),
            pltpu.VMEM((nf, pad, tf), f32),
        ],
        compiler_params=_params("parallel", "arbitrary", "arbitrary"),
        name="ffn",
    )(x3, prev, g_ffn, wg, wu, wd, cw, cb, g_fin)


def _time_major(a):
    return jnp.swapaxes(a, 0, 1).reshape(1, a.shape[0] * a.shape[1], a.shape[2])


def _batch_major(a, db):
    return jnp.swapaxes(a.reshape(a.shape[1] // db, db, a.shape[2]), 0, 1)


def kernel(x_prompt, x_sample, cache_kv_cmp, cache_kv_slc, cache_kv_win, state_rnn_h, state_rnn_conv, state_ffn_conv, page_table, norm_mix, w_in, rnn_conv_w, rnn_conv_b, rnn_wa, rnn_ba, rnn_wx, rnn_bx, rnn_lambda, cmp_w, cmp_pos, w_proj_rnn, w_proj_att, w_out, norm_ffn, ffn_w_gate, ffn_w_up, ffn_conv_w, ffn_conv_b, ffn_w_down, norm_final):
    B, T, _ = x_prompt.shape
    DB, Tq, _ = x_sample.shape
    depth = w_in.shape[0]
    n_pool, page = cache_kv_cmp.shape[1:3]
    n_pages = page_table.shape[1]
    past = n_pages * page
    wbuf = cache_kv_win.shape[2]
    tb = SUBLANES
    assert page % CMP_STRIDE == 0 and page % SLC_BLOCK == 0 and Tq >= RNN_CONV - 1 and Tq <= tb
    assert Tq & (Tq - 1) == 0 and DB % SUBLANES == 0 and T % 256 == 0
    kv_shape = (2, N_KV, HEAD_DIM)

    xp = x_prompt.reshape(B * T, D_MODEL)
    xs = _time_major(x_sample)[0]
    pst = [[] for _ in range(6)]
    sst = [[] for _ in range(6)]
    g_out = norm_final.reshape(1, D_MODEL)
    for l in range(depth):
        w = w_in[l]
        widths = (D_RNN, D_RNN, D_ATT, KV_W, KV_W, KV_W, 3 * N_HEADS, 2 * D_MODEL)
        cuts = [0]
        for wd_ in widths:
            cuts.append(cuts[-1] + wd_)
        part = lambda k: w[:, cuts[k]:cuts[k + 1]]
        w_cat = jnp.concatenate(
            [part(0), part(1), part(7), part(2), part(3), part(4), part(5),
             jnp.pad(part(6), ((0, 0), (0, GN_PAD - 3 * N_HEADS)))], axis=1).astype(bf16)
        g_mix = norm_mix[l].reshape(1, D_MODEL)
        cw, cb = rnn_conv_w[l], rnn_conv_b[l].reshape(1, D_RNN)
        wgate = jnp.concatenate([rnn_wa[l], rnn_wx[l]], axis=-1).astype(bf16)
        ba, bx, lam = (v[l].reshape(1, D_RNN) for v in (rnn_ba, rnn_bx, rnn_lambda))
        cmpw = cmp_w[l].astype(bf16)
        wc = jnp.concatenate([cmpw[:, :CMP_STRIDE].reshape(2, CMP_STRIDE * HEAD_DIM, HEAD_DIM),
                              cmpw[:, CMP_STRIDE:].reshape(2, CMP_STRIDE * HEAD_DIM, HEAD_DIM)], axis=-1)
        pw = cmpw.reshape(2, CMP_BLOCK * HEAD_DIM, HEAD_DIM)
        pos = cmp_pos[l].reshape(2, 1, CMP_BLOCK * HEAD_DIM)
        wr, wa, wo = w_proj_rnn[l].astype(bf16), w_proj_att[l].astype(bf16), w_out[l].astype(bf16)
        g_ffn = norm_ffn[l].reshape(1, D_MODEL)
        fwg, fwu, fwd = ffn_w_gate[l].astype(bf16), ffn_w_up[l].astype(bf16), ffn_w_down[l].astype(bf16)
        fcw, fcb = ffn_conv_w[l], ffn_conv_b[l].reshape(1, D_FF)
        last = l == depth - 1

        z = _inproj(xp, g_mix, w_cat)
        z3 = z.reshape(B, T, D_Z)
        kvc, kvs, kvw = (z3[:, :, c:c + KV_W] for c in (COL_KVC, COL_KVS, COL_KVW))
        y_rnn, h_last, rconv = _rglru(z3, jnp.zeros((B, RNN_CONV - 1, D_RNN), f32), jnp.zeros((B, 1, D_RNN), f32),
                                      cw, cb, wgate, ba, bx, lam, S=1, Tt=256)
        nc = T // CMP_STRIDE
        kc = _kc_call([kvc.reshape(B, nc, CMP_STRIDE * KV_W)],
                      [pl.BlockSpec((None, nc, CMP_STRIDE * KV_W), lambda b: (b, 0, 0))], B, nc, wc, pw, pos)
        o = jnp.concatenate([_attn_prompt(z3, kc, sb) for sb in range(T // SLC_CHUNK)], axis=1)
        x1 = _merge(xp, y_rnn.reshape(B * T, D_RNN), o.reshape(B * T, D_ATT), z, wr, wa, wo)
        y, fconv = _ffn(x1.reshape(B, T, D_MODEL), jnp.zeros((B, FFN_CONV - 1, D_FF), f32), g_ffn, fwg, fwu, fwd,
                        fcw, fcb, g_out, S=1, Tt=512, final_norm=last)
        xp = y.reshape(B * T, D_MODEL)
        wlen = min(WINDOW, T)
        for lst, v in zip(pst, (kvc.reshape(B, T, *kv_shape), kvs.reshape(B, T, *kv_shape),
                                kvw[:, T - wlen:].reshape(B, wlen, *kv_shape), h_last[:, 0], rconv, fconv[:, -1])):
            lst.append(v)

        zs = _inproj(xs, g_mix, w_cat)
        zs3 = zs.reshape(1, Tq * DB, D_Z)
        ys_rnn, hs_last, sconv = _rglru(zs3, _time_major(state_rnn_conv[l]), state_rnn_h[l][None],
                                        cw, cb, wgate, ba, bx, lam, S=DB, Tt=Tq)
        zq = _batch_major(zs3[:, :, COL_Q:], DB)
        zq = jnp.pad(zq, ((0, 0), (0, tb - Tq), (0, 0)))
        ncs = past // CMP_STRIDE
        rows_pp = page // CMP_STRIDE
        cmp_view = cache_kv_cmp[l].reshape(n_pool, rows_pp, CMP_STRIDE * KV_W)
        kcs = _kc_call(
            [cmp_view] * n_pages,
            [pl.BlockSpec((None, rows_pp, CMP_STRIDE * KV_W),
                          functools.partial(lambda b, pt, j: (pt[b, j], 0, 0), j=j)) for j in range(n_pages)],
            DB, ncs, wc, pw, pos, page_table=page_table)
        os_, s_win = _attn_sample(zq, kcs, cache_kv_win[l].reshape(DB, wbuf, KV_W),
                                  cache_kv_slc[l].reshape(n_pool, page, KV_W), page_table, tq=Tq)
        os_tm = _time_major(os_[:, :Tq])[0]
        xs1 = _merge(xs, ys_rnn[0], os_tm, zs, wr, wa, wo)
        ys, sfconv = _ffn(xs1[None], _time_major(state_ffn_conv[l]), g_ffn, fwg, fwu, fwd, fcw, fcb, g_out,
                          S=DB, Tt=Tq, final_norm=last)
        xs = ys[0]
        c0 = D_ATT
        for lst, v in zip(sst, (zq[:, :Tq, c0:c0 + KV_W].reshape(DB, Tq, *kv_shape),
                                zq[:, :Tq, c0 + KV_W:c0 + 2 * KV_W].reshape(DB, Tq, *kv_shape),
                                s_win.reshape(DB, wbuf, *kv_shape), hs_last[0],
                                _batch_major(sconv, DB), _batch_major(sfconv[:, -1], DB))):
            lst.append(v)

    y_prompt = xp.reshape(B, T, D_MODEL)
    y_sample = _batch_major(xs[None], DB)
    return (y_prompt, y_sample, *[jnp.stack(v) for v in pst], *[jnp.stack(v) for v in sst])
```

```python
import functools
import math

import jax
import jax.numpy as jnp
from jax import lax
from jax.experimental import pallas as pl
from jax.experimental.pallas import tpu as pltpu

D_MODEL = 2048
D_RNN = D_MODEL // 2
RNN_BLOCKS = 8
RNN_BLK = D_RNN // RNN_BLOCKS
RNN_CONV = 4
LRU_C = 8.0
HEAD_DIM = 128
D_ATT = D_MODEL // 2
N_HEADS = D_ATT // HEAD_DIM
N_KV = 2
HPG = N_HEADS // N_KV
KV_W = 2 * N_KV * HEAD_DIM
CMP_BLOCK = 32
CMP_STRIDE = 16
SLC_BLOCK = 64
TOPK = 8
WINDOW = 512
WIN_Q_BLOCK = 128
D_FF = 3 * D_MODEL
FFN_CONV = 3
RMS_EPS = 1e-6
NEG = -1e30

COL_XR = 0
COL_GR = COL_XR + D_RNN
COL_GM = COL_GR + D_RNN
COL_Q = COL_GM + 2 * D_MODEL
COL_KVC = COL_Q + D_ATT
COL_KVS = COL_KVC + KV_W
COL_KVW = COL_KVS + KV_W
COL_GN = COL_KVW + KV_W
GN_PAD = KV_W
D_Z = COL_GN + GN_PAD

VMEM_LIMIT_BYTES = 56 * 1024 * 1024
SUBLANES = 8
LANES = 128

WIN_KEYS = WINDOW + WIN_Q_BLOCK
KC_PITCH = CMP_STRIDE * 2 * N_KV + SUBLANES
SLC_CHUNK = 512

f32 = jnp.float32
bf16 = jnp.bfloat16


def _round_up(x, m):
    return (x + m - 1) // m * m


def _params(*sem):
    return pltpu.CompilerParams(dimension_semantics=sem, vmem_limit_bytes=VMEM_LIMIT_BYTES)


def _rms(x, g):
    return (x * lax.rsqrt(jnp.mean(x * x, axis=-1, keepdims=True) + RMS_EPS)) * g


def _gelu(x):
    return x * (0.5 * (1.0 + jnp.tanh(math.sqrt(2.0 / math.pi) * (x + 0.044715 * (x * x * x)))))


def _nt(a, b):
    return lax.dot_general(a, b, (((1,), (1,)), ((), ())), preferred_element_type=f32)


def _inproj_kernel(x_ref, g_ref, w_ref, o_ref, h_ref):
    @pl.when(pl.program_id(1) == 0)
    def _():
        h_ref[...] = _rms(x_ref[...], g_ref[...]).astype(bf16)

    o_ref[...] = jnp.dot(h_ref[...], w_ref[...], preferred_element_type=f32)


def _inproj(x2d, g, w_cat):
    rows = x2d.shape[0]
    tm = min(1024, rows)
    tn = 1536
    assert rows % tm == 0 and D_Z % tn == 0
    return pl.pallas_call(
        _inproj_kernel,
        grid=(rows // tm, D_Z // tn),
        in_specs=[
            pl.BlockSpec((tm, D_MODEL), lambda i, j: (i, 0)),
            pl.BlockSpec((1, D_MODEL), lambda i, j: (0, 0)),
            pl.BlockSpec((D_MODEL, tn), lambda i, j: (0, j)),
        ],
        out_specs=pl.BlockSpec((tm, tn), lambda i, j: (i, j)),
        out_shape=jax.ShapeDtypeStruct((rows, D_Z), f32),
        scratch_shapes=[pltpu.VMEM((tm, D_MODEL), bf16)],
        compiler_params=_params("parallel", "arbitrary"),
        name="inproj",
    )(x2d, g, w_cat)


def _rglru_kernel(xr_ref, gr_ref, prev_ref, h0_ref, cw_ref, cb_ref, wg_ref, ba_ref, bx_ref, lam_ref,
                  y_ref, hl_ref, nc_ref, xcat, hc, sa, su, *, S, Tt):
    rows = Tt * S
    k1 = RNN_CONV - 1
    pad = _round_up(k1 * S, SUBLANES)
    ipad = max((Tt // 2) * S, SUBLANES)

    @pl.when(pl.program_id(1) == 0)
    def _():
        xcat[pad - k1 * S:pad, :] = prev_ref[...]
        hc[...] = h0_ref[...]
        for k in range(2):
            sa[k, 0:ipad, :] = jnp.ones((ipad, D_RNN), f32)
            su[k, 0:ipad, :] = jnp.zeros((ipad, D_RNN), f32)

    xcat[pad:pad + rows, :] = xr_ref[...]
    xc = cb_ref[...] + cw_ref[k1:k1 + 1, :] * xr_ref[...]
    for j in range(k1):
        off = pad - (k1 - j) * S
        xc = xc + cw_ref[j:j + 1, :] * xcat[off:off + rows, :]
    tail = xcat[pad + rows - k1 * S:pad + rows, :]
    xcat[pad - k1 * S:pad, :] = tail
    nc_ref[...] = tail

    lam = lam_ref[...]
    softplus_neg_lam = jnp.maximum(-lam, 0.0) + jnp.log1p(jnp.exp(-jnp.abs(lam)))
    xcb = xc.astype(bf16)
    for n in range(RNN_BLOCKS):
        sl = slice(n * RNN_BLK, (n + 1) * RNN_BLK)
        gates = jnp.dot(xcb[:, sl], wg_ref[n], preferred_element_type=f32)
        r = jax.nn.sigmoid(gates[:, :RNN_BLK] + ba_ref[:, sl])
        i = jax.nn.sigmoid(gates[:, RNN_BLK:] + bx_ref[:, sl])
        log_a = (-LRU_C) * r * softplus_neg_lam[:, sl]
        a = jnp.exp(log_a)
        one_minus_a2 = -jnp.tanh(log_a) * (a * a + 1.0)
        sa[0, ipad:ipad + rows, sl] = a
        su[0, ipad:ipad + rows, sl] = jnp.sqrt(one_minus_a2) * (i * xc[:, sl])

    n_rounds = Tt.bit_length() - 1
    for rnd in range(n_rounds):
        d = (1 << rnd) * S
        src, dst = rnd % 2, 1 - rnd % 2
        a = sa[src, ipad:ipad + rows, :]
        su[dst, ipad:ipad + rows, :] = a * su[src, ipad - d:ipad - d + rows, :] + su[src, ipad:ipad + rows, :]
        sa[dst, ipad:ipad + rows, :] = a * sa[src, ipad - d:ipad - d + rows, :]
    fin = n_rounds % 2
    a_cum = sa[fin, ipad:ipad + rows, :]
    u_cum = su[fin, ipad:ipad + rows, :]
    if S == 1:
        h = a_cum * hc[...] + u_cum
    else:
        h = (a_cum.reshape(Tt, S, D_RNN) * hc[...][None] + u_cum.reshape(Tt, S, D_RNN)).reshape(rows, D_RNN)
    y_ref[...] = h * _gelu(gr_ref[...])
    su[fin, ipad:ipad + rows, :] = h
    last = su[fin, ipad + rows - S:ipad + rows, :]
    hc[...] = last
    hl_ref[...] = last


def _rglru(z3, prev, h0, cw, cb, wg, ba, bx, lam, *, S, Tt):
    nb, rper, _ = z3.shape
    rows = Tt * S
    assert Tt & (Tt - 1) == 0 and rper % rows == 0 and Tt >= RNN_CONV - 1
    k1 = RNN_CONV - 1
    pad = _round_up(k1 * S, SUBLANES)
    ipad = max((Tt // 2) * S, SUBLANES)
    vec = lambda: pl.BlockSpec((1, D_RNN), lambda b, t: (0, 0))
    return pl.pallas_call(
        functools.partial(_rglru_kernel, S=S, Tt=Tt),
        grid=(nb, rper // rows),
        in_specs=[
            pl.BlockSpec((None, rows, D_RNN), lambda b, t: (b, t, COL_XR // D_RNN)),
            pl.BlockSpec((None, rows, D_RNN), lambda b, t: (b, t, COL_GR // D_RNN)),
            pl.BlockSpec((None, k1 * S, D_RNN), lambda b, t: (b, 0, 0)),
            pl.BlockSpec((None, S, D_RNN), lambda b, t: (b, 0, 0)),
            pl.BlockSpec((RNN_CONV, D_RNN), lambda b, t: (0, 0)),
            vec(),
            pl.BlockSpec((RNN_BLOCKS, RNN_BLK, 2 * RNN_BLK), lambda b, t: (0, 0, 0)),
            vec(), vec(), vec(),
        ],
        out_specs=[
            pl.BlockSpec((None, rows, D_RNN), lambda b, t: (b, t, 0)),
            pl.BlockSpec((None, S, D_RNN), lambda b, t: (b, 0, 0)),
            pl.BlockSpec((None, k1 * S, D_RNN), lambda b, t: (b, 0, 0)),
        ],
        out_shape=[
            jax.ShapeDtypeStruct((nb, rper, D_RNN), f32),
            jax.ShapeDtypeStruct((nb, S, D_RNN), f32),
            jax.ShapeDtypeStruct((nb, k1 * S, D_RNN), f32),
        ],
        scratch_shapes=[
            pltpu.VMEM((pad + rows, D_RNN), f32),
            pltpu.VMEM((S, D_RNN), f32),
            pltpu.VMEM((2, ipad + rows, D_RNN), f32),
            pltpu.VMEM((2, ipad + rows, D_RNN), f32),
        ],
        compiler_params=_params("parallel", "arbitrary"),
        name="rglru",
    )(z3, z3, prev, h0, cw, cb, wg, ba, bx, lam)


def _kc_kernel(*refs, n_pieces):
    refs = refs[len(refs) - (n_pieces + 5):]
    x_refs = refs[:n_pieces]
    wc_ref, pw_ref, pos_ref, kc_ref, xp = refs[n_pieces:]
    per_tok = 2 * N_KV
    ch = CMP_STRIDE * per_tok
    nc = 0
    for r in x_refs:
        for k in range(r.shape[0] // ch):
            xp[nc * KC_PITCH:nc * KC_PITCH + ch, :] = r[k * ch:(k + 1) * ch, :]
            nc += 1
    for c in range(2):
        groups = [jnp.concatenate([xp[pl.ds(l * per_tok + c * N_KV + g, nc, stride=KC_PITCH), :]
                                   for l in range(CMP_STRIDE)], axis=1) for g in range(N_KV)]
        xs = jnp.concatenate(groups, axis=0).astype(bf16)
        fs = jnp.dot(xs, wc_ref[c], preferred_element_type=f32)
        pos = jnp.broadcast_to(pos_ref[c], (SUBLANES, CMP_BLOCK * HEAD_DIM)).astype(bf16)
        pos_bias = jnp.dot(pos, pw_ref[c], preferred_element_type=f32)[0:1, :]
        for g in range(N_KV):
            first = fs[g * nc:(g + 1) * nc, :HEAD_DIM]
            second = fs[g * nc:(g + 1) * nc, HEAD_DIM:]
            kc_ref[c * N_KV + g] = first + pltpu.roll(second, nc - 1, 0) + pos_bias


def _kc_call(x_arrays, x_specs, nseq, nc, wc, pw, pos, page_table=None):
    n_pieces = len(x_arrays)
    nsp = 0 if page_table is None else 1
    const = lambda shape: pl.BlockSpec(shape, lambda b, *_: (0,) * len(shape))
    grid_spec = pltpu.PrefetchScalarGridSpec(
        num_scalar_prefetch=nsp,
        grid=(nseq,),
        in_specs=list(x_specs) + [
            const((2, CMP_STRIDE * HEAD_DIM, 2 * HEAD_DIM)),
            const((2, CMP_BLOCK * HEAD_DIM, HEAD_DIM)),
            const((2, 1, CMP_BLOCK * HEAD_DIM)),
        ],
        out_specs=pl.BlockSpec((None, 2 * N_KV, nc, HEAD_DIM), lambda b, *_: (b, 0, 0, 0)),
        scratch_shapes=[pltpu.VMEM((nc * KC_PITCH, HEAD_DIM), f32)],
    )
    args = ([] if page_table is None else [page_table]) + list(x_arrays) + [wc, pw, pos]
    return pl.pallas_call(
        functools.partial(_kc_kernel, n_pieces=n_pieces),
        grid_spec=grid_spec,
        out_shape=jax.ShapeDtypeStruct((nseq, 2 * N_KV, nc, HEAD_DIM), f32),
        compiler_params=_params("parallel"),
        name="kc_sample" if nsp else "kc_prompt",
    )(*args)


def _pos_cols(n_rows, step):
    r = lax.broadcasted_iota(jnp.int32, (n_rows, HEAD_DIM), 0) * step
    lane = lax.broadcasted_iota(jnp.int32, (n_rows, HEAD_DIM), 1)
    low_bits = SLC_BLOCK.bit_length() - 1
    hi = (r >> low_bits) << low_bits
    return jnp.where(lane == 0, hi, jnp.where(lane == 1, r - hi, 0)).astype(f32).astype(bf16)


def _keys_with_pos(k, step=1):
    return jnp.concatenate([k.astype(bf16), _pos_cols(k.shape[0], step)], axis=1)


def _ones_col(n_rows):
    lane = lax.broadcasted_iota(jnp.int32, (n_rows, HEAD_DIM), 1)
    return jnp.where(lane == 0, 1.0, 0.0).astype(bf16)


def _vals_with_ones(v):
    return jnp.concatenate([v.astype(bf16), _ones_col(v.shape[0])], axis=1)


def _exp_rows(s, mask_bias):
    s = s + mask_bias
    m = jnp.max(s, axis=-1, keepdims=True)
    e = jnp.exp(s - m)
    inv = jnp.where(m > 0.5 * NEG, 1.0 / jnp.maximum(jnp.sum(e, axis=-1, keepdims=True), 1e-30), 0.0)
    return e, inv


def _select_blocks(psum, qpos, tb):
    lanes_per_blk = SLC_BLOCK // CMP_STRIDE
    n_lane = psum.shape[1]
    imp = psum + pltpu.roll(psum, 1, 1)
    for back in range(1, lanes_per_blk):
        imp = imp + pltpu.roll(psum, n_lane - back, 1)
    lane = lax.broadcasted_iota(jnp.int32, (tb, n_lane), 1)
    lane_f = lane.astype(f32)
    blk = lane >> (lanes_per_blk.bit_length() - 1)
    cur = qpos >> (SLC_BLOCK.bit_length() - 1)
    cand = ((lane & (lanes_per_blk - 1)) == 0) & (blk >= 1) & (blk < cur)
    score = jnp.where(cand, imp, NEG)
    sel = (lane == 0) & (cur > 0)
    for _ in range(TOPK - 2):
        best = jnp.max(score, axis=-1, keepdims=True)
        first = jnp.min(jnp.where(score == best, lane_f, float(n_lane)), axis=-1, keepdims=True)
        pick = (lane_f == first) & (best > 0.5 * NEG)
        sel = sel | pick
        score = jnp.where(pick, NEG, score)
    return jnp.where(sel, 1.0, 0.0).astype(bf16)


def _attn_core(q, qpos, sg, kc_ref, sk_ref, sv_ref, wk_ref, wv_ref, *, tb, slc_chunks, win_row0, win_pos0):
    scale = 1.0 / math.sqrt(HEAD_DIM)
    n_cmp = kc_ref.shape[1]
    slc_shift = SLC_BLOCK.bit_length() - 1
    cur = qpos >> slc_shift
    lane_c = lax.broadcasted_iota(jnp.int32, (tb, n_cmp), 1)
    bias_c = jnp.where(qpos - (CMP_STRIDE * lane_c + (CMP_BLOCK - 1)) >= 0, 0.0, NEG)
    lane_q = lax.broadcasted_iota(jnp.int32, (tb, HEAD_DIM), 1)
    hs = lambda x, p: x[p * tb:(p + 1) * tb]
    outs = []
    for g in range(N_KV):
        parts = []
        for p in range(HPG):
            h = g * HPG + p
            slope = 2.0 ** (-8.0 * (h + 1) / N_HEADS)
            parts.append(jnp.concatenate([q[:, h * HEAD_DIM:(h + 1) * HEAD_DIM] * scale,
                                          jnp.where(lane_q < 2, slope, 0.0)], axis=1))
        qg = jnp.concatenate(parts, axis=0).astype(bf16)

        s = _nt(qg, _keys_with_pos(kc_ref[g], CMP_STRIDE))
        ps = []
        for p in range(HPG):
            e, inv = _exp_rows(hs(s, p), bias_c)
            ps.append(e * inv)
        o_cmp = jnp.dot(jnp.concatenate(ps, axis=0).astype(bf16), kc_ref[N_KV + g].astype(bf16),
                        preferred_element_type=f32)
        selm = _select_blocks(ps[0] + ps[1] + ps[2] + ps[3], qpos, tb)

        def slc_chunk(k0, width, st):
            m, acc = st
            s = _nt(qg, sk_ref[g, pl.ds(k0, width), :])
            kpos = k0 + lax.broadcasted_iota(jnp.int32, (1, width), 1)
            kblk = kpos >> slc_shift
            hot = lax.broadcasted_iota(jnp.int32, (n_cmp, width), 0) == kblk * (SLC_BLOCK // CMP_STRIDE)
            chosen = jnp.dot(selm, jnp.where(hot, 1.0, 0.0).astype(bf16), preferred_element_type=f32)
            bias = jnp.where((chosen > 0.5) | ((kblk == cur) & (qpos - kpos >= 0)), 0.0, NEG)
            ms, es, alphas = [], [], []
            for p in range(HPG):
                sp = hs(s, p) + bias
                m_new = jnp.maximum(hs(m, p), jnp.max(sp, axis=-1, keepdims=True))
                alphas.append(jnp.exp(hs(m, p) - m_new))
                es.append(jnp.exp(sp - m_new))
                ms.append(m_new)
            pv = jnp.dot(jnp.concatenate(es, axis=0).astype(bf16), sv_ref[g, pl.ds(k0, width), :],
                         preferred_element_type=f32)
            return jnp.concatenate(ms, axis=0), jnp.concatenate(alphas, axis=0) * acc + pv

        st = (jnp.full((HPG * tb, 1), NEG, f32), jnp.zeros((HPG * tb, 2 * HEAD_DIM), f32))
        for k0, width in slc_chunks:
            st = slc_chunk(k0, width, st)
        o_slc = st[1][:, :HEAD_DIM] / jnp.maximum(st[1][:, HEAD_DIM:HEAD_DIM + 1], 1e-30)

        s = _nt(qg, wk_ref[g, pl.ds(win_row0, WIN_KEYS), :])
        kpos = win_pos0 + lax.broadcasted_iota(jnp.int32, (1, WIN_KEYS), 1)
        d = qpos - kpos
        bias_w = jnp.where((d >= 0) & (d <= WINDOW) & (kpos >= 0), 0.0, NEG)
        es = []
        for p in range(HPG):
            sp = hs(s, p) + bias_w
            es.append(jnp.exp(sp - jnp.max(sp, axis=-1, keepdims=True)))
        pv = jnp.dot(jnp.concatenate(es, axis=0).astype(bf16), wv_ref[g, pl.ds(win_row0, WIN_KEYS), :],
                     preferred_element_type=f32)
        o_win = pv[:, :HEAD_DIM] / jnp.maximum(pv[:, HEAD_DIM:HEAD_DIM + 1], 1e-30)

        for p in range(HPG):
            h = g * HPG + p
            outs.append(sg[:, h:h + 1] * hs(o_cmp, p) + sg[:, N_HEADS + h:N_HEADS + h + 1] * hs(o_slc, p)
                        + sg[:, 2 * N_HEADS + h:2 * N_HEADS + h + 1] * hs(o_win, p))
    return jnp.concatenate(outs, axis=1)


def _attn_scratch(slc_rows, win_rows):
    return [pltpu.VMEM((N_KV, rows, 2 * HEAD_DIM), bf16) for rows in (slc_rows, slc_rows, win_rows, win_rows)]


def _attn_prompt_kernel(q_ref, kvs_ref, kvw_ref, gn_ref, kc_ref, o_ref, sk, sv, wk, wv, *, T, sb):
    tb = WIN_Q_BLOCK
    i = sb * (SLC_CHUNK // tb) + pl.program_id(1)

    @pl.when(pl.program_id(1) == 0)
    def _():
        for g in range(N_KV):
            for src, kdst, vdst in ((kvs_ref, sk, sv), (kvw_ref, wk, wv)):
                kdst[g] = _keys_with_pos(src[:, g * HEAD_DIM:(g + 1) * HEAD_DIM])
                vdst[g] = _vals_with_ones(src[:, (N_KV + g) * HEAD_DIM:(N_KV + g + 1) * HEAD_DIM])

    qpos = i * tb + lax.broadcasted_iota(jnp.int32, (tb, 1), 0)
    win0 = pl.multiple_of(jnp.clip(i * tb - WINDOW, 0, T - WIN_KEYS), LANES)
    o_ref[...] = _attn_core(
        q_ref[...], qpos, jax.nn.sigmoid(gn_ref[:, 0:LANES]), kc_ref, sk, sv, wk, wv, tb=tb,
        slc_chunks=[(k0, SLC_CHUNK) for k0 in range(0, (sb + 1) * SLC_CHUNK, SLC_CHUNK)],
        win_row0=win0, win_pos0=win0)


def _attn_prompt(z3, kc, sb):
    B, T, _ = z3.shape
    assert T % SLC_CHUNK == 0 and T >= WIN_KEYS
    tb = WIN_Q_BLOCK
    nq = SLC_CHUNK // tb
    return pl.pallas_call(
        functools.partial(_attn_prompt_kernel, T=T, sb=sb),
        grid=(B, nq),
        in_specs=[
            pl.BlockSpec((None, tb, D_ATT), lambda b, i: (b, sb * nq + i, COL_Q // D_ATT)),
            pl.BlockSpec((None, T, KV_W), lambda b, i: (b, 0, COL_KVS // KV_W)),
            pl.BlockSpec((None, T, KV_W), lambda b, i: (b, 0, COL_KVW // KV_W)),
            pl.BlockSpec((None, tb, GN_PAD), lambda b, i: (b, sb * nq + i, COL_GN // GN_PAD)),
            pl.BlockSpec((None, 2 * N_KV, T // CMP_STRIDE, HEAD_DIM), lambda b, i: (b, 0, 0, 0)),
        ],
        out_specs=pl.BlockSpec((None, tb, D_ATT), lambda b, i: (b, i, 0)),
        out_shape=jax.ShapeDtypeStruct((B, SLC_CHUNK, D_ATT), f32),
        scratch_shapes=_attn_scratch(T, T),
        compiler_params=_params("parallel", "arbitrary"),
        name=f"attn_prompt_{sb}",
    )(z3, z3, z3, z3, kc)


def _attn_sample_kernel(*refs, n_pages, page, tq, tb):
    zq_ref, kc_ref, win_ref = refs[1:4]
    page_refs = refs[4:4 + n_pages]
    o_ref, skw_ref, sk, sv, wk, wv = refs[4 + n_pages:]
    per_tok = 2 * N_KV
    past = n_pages * page
    wbuf = win_ref.shape[0] // per_tok
    q0 = 0
    c_kvs = q0 + D_ATT + KV_W
    c_kvw = c_kvs + KV_W
    c_gn = c_kvw + KV_W

    def new_rows(col):
        return jnp.concatenate([zq_ref[:, col:col + HEAD_DIM], jnp.zeros((LANES - tb, HEAD_DIM), f32)],
                               axis=0).astype(bf16)

    for g in range(N_KV):
        for j in range(n_pages):
            rows = slice(j * page, (j + 1) * page)
            sk[g, rows, 0:HEAD_DIM] = page_refs[j][pl.ds(g, page, stride=per_tok), :].astype(bf16)
            sv[g, rows, 0:HEAD_DIM] = page_refs[j][pl.ds(N_KV + g, page, stride=per_tok), :].astype(bf16)
        sk[g, past:past + LANES, 0:HEAD_DIM] = new_rows(c_kvs + g * HEAD_DIM)
        sv[g, past:past + LANES, 0:HEAD_DIM] = new_rows(c_kvs + (N_KV + g) * HEAD_DIM)
        sk[g, :, HEAD_DIM:] = _pos_cols(past + LANES, 1)
        sv[g, :, HEAD_DIM:] = _ones_col(past + LANES)
        wk[g, 0:wbuf, 0:HEAD_DIM] = win_ref[pl.ds(g, wbuf, stride=per_tok), :].astype(bf16)
        wv[g, 0:wbuf, 0:HEAD_DIM] = win_ref[pl.ds(N_KV + g, wbuf, stride=per_tok), :].astype(bf16)
        wk[g, wbuf:wbuf + LANES, 0:HEAD_DIM] = new_rows(c_kvw + g * HEAD_DIM)
        wv[g, wbuf:wbuf + LANES, 0:HEAD_DIM] = new_rows(c_kvw + (N_KV + g) * HEAD_DIM)
        wk[g, :, HEAD_DIM:] = _pos_cols(wbuf + LANES, 1)
        wv[g, :, HEAD_DIM:] = _ones_col(wbuf + LANES)
    skw_ref[0:(wbuf - tq) * per_tok, :] = win_ref[tq * per_tok:wbuf * per_tok, :]
    for t in range(tq):
        for cg in range(per_tok):
            r = (wbuf - tq + t) * per_tok + cg
            skw_ref[r:r + 1, :] = zq_ref[t:t + 1, c_kvw + cg * HEAD_DIM:c_kvw + (cg + 1) * HEAD_DIM]

    qpos = past + lax.broadcasted_iota(jnp.int32, (tb, 1), 0)
    o_ref[...] = _attn_core(
        zq_ref[:, q0:q0 + D_ATT], qpos, jax.nn.sigmoid(zq_ref[:, c_gn:c_gn + LANES]), kc_ref, sk, sv, wk, wv,
        tb=tb, slc_chunks=[(0, past + LANES)], win_row0=0, win_pos0=past - wbuf)


def _attn_sample(zq, kc, cache_win, cache_slc, page_table, *, tq):
    DB, tb, zw = zq.shape
    per_tok = 2 * N_KV
    n_pages = page_table.shape[1]
    page = cache_slc.shape[1] // per_tok
    past = n_pages * page
    wbuf = cache_win.shape[1] // per_tok
    assert wbuf + LANES == WIN_KEYS and tq <= tb and tq < CMP_STRIDE and (tq * per_tok) % SUBLANES == 0
    page_specs = [pl.BlockSpec((None, page * per_tok, HEAD_DIM),
                               functools.partial(lambda b, pt, j: (pt[b, j], 0, 0), j=j)) for j in range(n_pages)]
    grid_spec = pltpu.PrefetchScalarGridSpec(
        num_scalar_prefetch=1,
        grid=(DB,),
        in_specs=[
            pl.BlockSpec((None, tb, zw), lambda b, pt: (b, 0, 0)),
            pl.BlockSpec((None, 2 * N_KV, past // CMP_STRIDE, HEAD_DIM), lambda b, pt: (b, 0, 0, 0)),
            pl.BlockSpec((None, wbuf * per_tok, HEAD_DIM), lambda b, pt: (b, 0, 0)),
        ] + page_specs,
        out_specs=[
            pl.BlockSpec((None, tb, D_ATT), lambda b, pt: (b, 0, 0)),
            pl.BlockSpec((None, wbuf * per_tok, HEAD_DIM), lambda b, pt: (b, 0, 0)),
        ],
        scratch_shapes=_attn_scratch(past + LANES, wbuf + LANES),
    )
    return pl.pallas_call(
        functools.partial(_attn_sample_kernel, n_pages=n_pages, page=page, tq=tq, tb=tb),
        grid_spec=grid_spec,
        out_shape=[jax.ShapeDtypeStruct((DB, tb, D_ATT), f32),
                   jax.ShapeDtypeStruct((DB, wbuf * per_tok, HEAD_DIM), f32)],
        compiler_params=_params("parallel"),
        name="attn_sample",
    )(page_table, zq, kc, cache_win, *([cache_slc] * n_pages))


def _merge_kernel(x_ref, yr_ref, o_ref, gm1_ref, gm2_ref, wr_ref, wa_ref, wo_ref, out_ref):
    m = (jax.nn.sigmoid(gm1_ref[...]) * jnp.dot(yr_ref[...].astype(bf16), wr_ref[...], preferred_element_type=f32)
         + jax.nn.sigmoid(gm2_ref[...]) * jnp.dot(o_ref[...].astype(bf16), wa_ref[...], preferred_element_type=f32))
    out_ref[...] = x_ref[...] + jnp.dot(m.astype(bf16), wo_ref[...], preferred_element_type=f32)


def _merge(x2d, yr, o, z2d, wr, wa, wo):
    rows = x2d.shape[0]
    tm = min(256, rows)
    const = lambda shape: pl.BlockSpec(shape, lambda i: (0, 0), pipeline_mode=pl.Buffered(1))
    return pl.pallas_call(
        _merge_kernel,
        grid=(rows // tm,),
        in_specs=[
            pl.BlockSpec((tm, D_MODEL), lambda i: (i, 0)),
            pl.BlockSpec((tm, D_RNN), lambda i: (i, 0)),
            pl.BlockSpec((tm, D_ATT), lambda i: (i, 0)),
            pl.BlockSpec((tm, D_MODEL), lambda i: (i, COL_GM // D_MODEL)),
            pl.BlockSpec((tm, D_MODEL), lambda i: (i, COL_GM // D_MODEL + 1)),
            const((D_RNN, D_MODEL)), const((D_ATT, D_MODEL)), const((D_MODEL, D_MODEL)),
        ],
        out_specs=pl.BlockSpec((tm, D_MODEL), lambda i: (i, 0)),
        out_shape=jax.ShapeDtypeStruct((rows, D_MODEL), f32),
        compiler_params=_params("parallel"),
        name="merge",
    )(x2d, yr, o, z2d, z2d, wr, wa, wo)


def _ffn_kernel(x_ref, prev_ref, gf_ref, wg_ref, wu_ref, wd_ref, cw_ref, cb_ref, gl_ref,
                y_ref, fc_ref, h_ref, acc_ref, gcat, carry, *, S, rows, final_norm):
    ti, fi = pl.program_id(1), pl.program_id(2)
    k1 = FFN_CONV - 1
    pad = _round_up(k1 * S, SUBLANES)

    @pl.when(fi == 0)
    def _():
        h_ref[...] = _rms(x_ref[...], gf_ref[...]).astype(bf16)
        acc_ref[...] = jnp.zeros_like(acc_ref)

    @pl.when(ti == 0)
    def _():
        carry[fi, pad - k1 * S:pad, :] = prev_ref[...]

    h = h_ref[...]
    gate = jnp.dot(h, wg_ref[...], preferred_element_type=f32)
    gcat[pad - k1 * S:pad, :] = carry[fi, pad - k1 * S:pad, :]
    gcat[pad:pad + rows, :] = gate
    u = cb_ref[...] + cw_ref[k1:k1 + 1, :] * gate
    for j in range(k1):
        off = pad - (k1 - j) * S
        u = u + cw_ref[j:j + 1, :] * gcat[off:off + rows, :]
    tail = gcat[pad + rows - k1 * S:pad + rows, :]
    carry[fi, pad - k1 * S:pad, :] = tail
    fc_ref[...] = tail
    up = jnp.dot(h, wu_ref[...], preferred_element_type=f32)
    acc_ref[...] += jnp.dot((_gelu(u) * up).astype(bf16), wd_ref[...], preferred_element_type=f32)

    @pl.when(fi == pl.num_programs(2) - 1)
    def _():
        out = x_ref[...] + acc_ref[...]
        y_ref[...] = _rms(out, gl_ref[...]) if final_norm else out


def _ffn(x3, prev, g_ffn, wg, wu, wd, cw, cb, g_fin, *, S, Tt, final_norm):
    nb, rper, _ = x3.shape
    rows = Tt * S
    tf = 512
    nf = D_FF // tf
    k1 = FFN_CONV - 1
    pad = _round_up(k1 * S, SUBLANES)
    assert rper % rows == 0 and Tt >= k1
    return pl.pallas_call(
        functools.partial(_ffn_kernel, S=S, rows=rows, final_norm=final_norm),
        grid=(nb, rper // rows, nf),
        in_specs=[
            pl.BlockSpec((None, rows, D_MODEL), lambda b, t, f: (b, t, 0)),
            pl.BlockSpec((None, k1 * S, tf), lambda b, t, f: (b, 0, f)),
            pl.BlockSpec((1, D_MODEL), lambda b, t, f: (0, 0)),
            pl.BlockSpec((D_MODEL, tf), lambda b, t, f: (0, f)),
            pl.BlockSpec((D_MODEL, tf), lambda b, t, f: (0, f)),
            pl.BlockSpec((tf, D_MODEL), lambda b, t, f: (f, 0)),
            pl.BlockSpec((FFN_CONV, tf), lambda b, t, f: (0, f)),
            pl.BlockSpec((1, tf), lambda b, t, f: (0, f)),
            pl.BlockSpec((1, D_MODEL), lambda b, t, f: (0, 0)),
        ],
        out_specs=[
            pl.BlockSpec((None, rows, D_MODEL), lambda b, t, f: (b, t, 0)),
            pl.BlockSpec((None, None, k1 * S, tf), lambda b, t, f: (b, t, 0, f)),
        ],
        out_shape=[
            jax.ShapeDtypeStruct((nb, rper, D_MODEL), f32),
            jax.ShapeDtypeStruct((nb, rper // rows, k1 * S, D_FF), f32),
        ],
        scratch_shapes=[
            pltpu.VMEM((rows, D_MODEL), bf16),
            pltpu.VMEM((rows, D_MODEL), f32),
            pltpu.VMEM((pad + rows, tf), f32),
            pltpu.VMEM((nf, pad, tf), f32),
        ],
        compiler_params=_params("parallel", "arbitrary", "arbitrary"),
        name="ffn",
    )(x3, prev, g_ffn, wg, wu, wd, cw, cb, g_fin)


def _time_major(a):
    return jnp.swapaxes(a, 0, 1).reshape(1, a.shape[0] * a.shape[1], a.shape[2])


def _batch_major(a, db):
    return jnp.swapaxes(a.reshape(a.shape[1] // db, db, a.shape[2]), 0, 1)


def kernel(x_prompt, x_sample, cache_kv_cmp, cache_kv_slc, cache_kv_win, state_rnn_h, state_rnn_conv, state_ffn_conv, page_table, norm_mix, w_in, rnn_conv_w, rnn_conv_b, rnn_wa, rnn_ba, rnn_wx, rnn_bx, rnn_lambda, cmp_w, cmp_pos, w_proj_rnn, w_proj_att, w_out, norm_ffn, ffn_w_gate, ffn_w_up, ffn_conv_w, ffn_conv_b, ffn_w_down, norm_final):
    B, T, _ = x_prompt.shape
    DB, Tq, _ = x_sample.shape
    depth = w_in.shape[0]
    n_pool, page = cache_kv_cmp.shape[1:3]
    n_pages = page_table.shape[1]
    past = n_pages * page
    wbuf = cache_kv_win.shape[2]
    tb = SUBLANES
    assert page % CMP_STRIDE == 0 and page % SLC_BLOCK == 0 and Tq >= RNN_CONV - 1 and Tq <= tb
    assert Tq & (Tq - 1) == 0 and DB % SUBLANES == 0 and T % 256 == 0
    kv_shape = (2, N_KV, HEAD_DIM)

    xp = x_prompt.reshape(B * T, D_MODEL)
    xs = _time_major(x_sample)[0]
    pst = [[] for _ in range(6)]
    sst = [[] for _ in range(6)]
    g_out = norm_final.reshape(1, D_MODEL)
    for l in range(depth):
        w = w_in[l]
        widths = (D_RNN, D_RNN, D_ATT, KV_W, KV_W, KV_W, 3 * N_HEADS, 2 * D_MODEL)
        cuts = [0]
        for wd_ in widths:
            cuts.append(cuts[-1] + wd_)
        part = lambda k: w[:, cuts[k]:cuts[k + 1]]
        w_cat = jnp.concatenate(
            [part(0), part(1), part(7), part(2), part(3), part(4), part(5),
             jnp.pad(part(6), ((0, 0), (0, GN_PAD - 3 * N_HEADS)))], axis=1).astype(bf16)
        g_mix = norm_mix[l].reshape(1, D_MODEL)
        cw, cb = rnn_conv_w[l], rnn_conv_b[l].reshape(1, D_RNN)
        wgate = jnp.concatenate([rnn_wa[l], rnn_wx[l]], axis=-1).astype(bf16)
        ba, bx, lam = (v[l].reshape(1, D_RNN) for v in (rnn_ba, rnn_bx, rnn_lambda))
        cmpw = cmp_w[l].astype(bf16)
        wc = jnp.concatenate([cmpw[:, :CMP_STRIDE].reshape(2, CMP_STRIDE * HEAD_DIM, HEAD_DIM),
                              cmpw[:, CMP_STRIDE:].reshape(2, CMP_STRIDE * HEAD_DIM, HEAD_DIM)], axis=-1)
        pw = cmpw.reshape(2, CMP_BLOCK * HEAD_DIM, HEAD_DIM)
        pos = cmp_pos[l].reshape(2, 1, CMP_BLOCK * HEAD_DIM)
        wr, wa, wo = w_proj_rnn[l].astype(bf16), w_proj_att[l].astype(bf16), w_out[l].astype(bf16)
        g_ffn = norm_ffn[l].reshape(1, D_MODEL)
        fwg, fwu, fwd = ffn_w_gate[l].astype(bf16), ffn_w_up[l].astype(bf16), ffn_w_down[l].astype(bf16)
        fcw, fcb = ffn_conv_w[l], ffn_conv_b[l].reshape(1, D_FF)
        last = l == depth - 1

        z = _inproj(xp, g_mix, w_cat)
        z3 = z.reshape(B, T, D_Z)
        kvc, kvs, kvw = (z3[:, :, c:c + KV_W] for c in (COL_KVC, COL_KVS, COL_KVW))
        y_rnn, h_last, rconv = _rglru(z3, jnp.zeros((B, RNN_CONV - 1, D_RNN), f32), jnp.zeros((B, 1, D_RNN), f32),
                                      cw, cb, wgate, ba, bx, lam, S=1, Tt=256)
        per_tok = 2 * N_KV
        kvc_out = kvc.reshape(B, T, *kv_shape)
        kc = _kc_call([kvc_out.reshape(B, T * per_tok, HEAD_DIM)],
                      [pl.BlockSpec((None, T * per_tok, HEAD_DIM), lambda b: (b, 0, 0))],
                      B, T // CMP_STRIDE, wc, pw, pos)
        o = jnp.concatenate([_attn_prompt(z3, kc, sb) for sb in range(T // SLC_CHUNK)], axis=1)
        x1 = _merge(xp, y_rnn.reshape(B * T, D_RNN), o.reshape(B * T, D_ATT), z, wr, wa, wo)
        y, fconv = _ffn(x1.reshape(B, T, D_MODEL), jnp.zeros((B, FFN_CONV - 1, D_FF), f32), g_ffn, fwg, fwu, fwd,
                        fcw, fcb, g_out, S=1, Tt=512, final_norm=last)
        xp = y.reshape(B * T, D_MODEL)
        wlen = min(WINDOW, T)
        for lst, v in zip(pst, (kvc_out, kvs.reshape(B, T, *kv_shape),
                                kvw[:, T - wlen:].reshape(B, wlen, *kv_shape), h_last[:, 0], rconv, fconv[:, -1])):
            lst.append(v)

        zs = _inproj(xs, g_mix, w_cat)
        zs3 = zs.reshape(1, Tq * DB, D_Z)
        ys_rnn, hs_last, sconv = _rglru(zs3, _time_major(state_rnn_conv[l]), state_rnn_h[l][None],
                                        cw, cb, wgate, ba, bx, lam, S=DB, Tt=Tq)
        zq = _batch_major(zs3[:, :, COL_Q:], DB)
        zq = jnp.pad(zq, ((0, 0), (0, tb - Tq), (0, 0)))
        kcs = _kc_call(
            [cache_kv_cmp[l].reshape(n_pool, page * per_tok, HEAD_DIM)] * n_pages,
            [pl.BlockSpec((None, page * per_tok, HEAD_DIM),
                          functools.partial(lambda b, pt, j: (pt[b, j], 0, 0), j=j)) for j in range(n_pages)],
            DB, past // CMP_STRIDE, wc, pw, pos, page_table=page_table)
        os_, s_win = _attn_sample(zq, kcs, cache_kv_win[l].reshape(DB, wbuf * per_tok, HEAD_DIM),
                                  cache_kv_slc[l].reshape(n_pool, page * per_tok, HEAD_DIM), page_table, tq=Tq)
        os_tm = _time_major(os_[:, :Tq])[0]
        xs1 = _merge(xs, ys_rnn[0], os_tm, zs, wr, wa, wo)
        ys, sfconv = _ffn(xs1[None], _time_major(state_ffn_conv[l]), g_ffn, fwg, fwu, fwd, fcw, fcb, g_out,
                          S=DB, Tt=Tq, final_norm=last)
        xs = ys[0]
        c0 = D_ATT
        for lst, v in zip(sst, (zq[:, :Tq, c0:c0 + KV_W].reshape(DB, Tq, *kv_shape),
                                zq[:, :Tq, c0 + KV_W:c0 + 2 * KV_W].reshape(DB, Tq, *kv_shape),
                                s_win.reshape(DB, wbuf, *kv_shape), hs_last[0],
                                _batch_major(sconv, DB), _batch_major(sfconv[:, -1], DB))):
            lst.append(v)

    y_prompt = xp.reshape(B, T, D_MODEL)
    y_sample = _batch_major(xs[None], DB)
    return (y_prompt, y_sample, *[jnp.stack(v) for v in pst], *[jnp.stack(v) for v in sst])
```

```python
import functools
import math

import jax
import jax.numpy as jnp
from jax import lax
from jax.experimental import pallas as pl
from jax.experimental.pallas import tpu as pltpu

D_MODEL = 2048
D_RNN = D_MODEL // 2
RNN_BLOCKS = 8
RNN_BLK = D_RNN // RNN_BLOCKS
RNN_CONV = 4
LRU_C = 8.0
HEAD_DIM = 128
D_ATT = D_MODEL // 2
N_HEADS = D_ATT // HEAD_DIM
N_KV = 2
HPG = N_HEADS // N_KV
KV_W = 2 * N_KV * HEAD_DIM
CMP_BLOCK = 32
CMP_STRIDE = 16
SLC_BLOCK = 64
TOPK = 8
WINDOW = 512
WIN_Q_BLOCK = 128
D_FF = 3 * D_MODEL
FFN_CONV = 3
RMS_EPS = 1e-6
NEG = -1e30

COL_XR = 0
COL_GR = COL_XR + D_RNN
COL_GM = COL_GR + D_RNN
COL_Q = COL_GM + 2 * D_MODEL
COL_KVC = COL_Q + D_ATT
COL_KVS = COL_KVC + KV_W
COL_KVW = COL_KVS + KV_W
COL_GN = COL_KVW + KV_W
GN_PAD = KV_W
D_Z = COL_GN + GN_PAD

VMEM_LIMIT_BYTES = 56 * 1024 * 1024
SUBLANES = 8
LANES = 128

WIN_KEYS = WINDOW + WIN_Q_BLOCK
KC_PITCH = CMP_STRIDE * 2 * N_KV + SUBLANES
SLC_CHUNK = 512

f32 = jnp.float32
bf16 = jnp.bfloat16


def _round_up(x, m):
    return (x + m - 1) // m * m


def _params(*sem):
    return pltpu.CompilerParams(dimension_semantics=sem, vmem_limit_bytes=VMEM_LIMIT_BYTES)


def _rms(x, g):
    return (x * lax.rsqrt(jnp.mean(x * x, axis=-1, keepdims=True) + RMS_EPS)) * g


def _gelu(x):
    return x * (0.5 * (1.0 + jnp.tanh(math.sqrt(2.0 / math.pi) * (x + 0.044715 * (x * x * x)))))


def _nt(a, b):
    return lax.dot_general(a, b, (((1,), (1,)), ((), ())), preferred_element_type=f32)


def _inproj_kernel(x_ref, g_ref, w_ref, o_ref, h_ref):
    @pl.when(pl.program_id(1) == 0)
    def _():
        h_ref[...] = _rms(x_ref[...], g_ref[...]).astype(bf16)

    o_ref[...] = jnp.dot(h_ref[...], w_ref[...], preferred_element_type=f32)


def _inproj(x2d, g, w_cat):
    rows = x2d.shape[0]
    tm = min(1024, rows)
    tn = 1536
    assert rows % tm == 0 and D_Z % tn == 0
    return pl.pallas_call(
        _inproj_kernel,
        grid=(rows // tm, D_Z // tn),
        in_specs=[
            pl.BlockSpec((tm, D_MODEL), lambda i, j: (i, 0)),
            pl.BlockSpec((1, D_MODEL), lambda i, j: (0, 0)),
            pl.BlockSpec((D_MODEL, tn), lambda i, j: (0, j)),
        ],
        out_specs=pl.BlockSpec((tm, tn), lambda i, j: (i, j)),
        out_shape=jax.ShapeDtypeStruct((rows, D_Z), f32),
        scratch_shapes=[pltpu.VMEM((tm, D_MODEL), bf16)],
        compiler_params=_params("parallel", "arbitrary"),
        name="inproj",
    )(x2d, g, w_cat)


def _rglru_kernel(xr_ref, gr_ref, prev_ref, h0_ref, cw_ref, cb_ref, wg_ref, ba_ref, bx_ref, lam_ref,
                  y_ref, hl_ref, nc_ref, xcat, hc, sa, su, *, S, Tt):
    rows = Tt * S
    k1 = RNN_CONV - 1
    pad = _round_up(k1 * S, SUBLANES)
    ipad = max((Tt // 2) * S, SUBLANES)

    @pl.when(pl.program_id(1) == 0)
    def _():
        xcat[pad - k1 * S:pad, :] = prev_ref[...]
        hc[...] = h0_ref[...]
        for k in range(2):
            sa[k, 0:ipad, :] = jnp.ones((ipad, D_RNN), f32)
            su[k, 0:ipad, :] = jnp.zeros((ipad, D_RNN), f32)

    xcat[pad:pad + rows, :] = xr_ref[...]
    xc = cb_ref[...] + cw_ref[k1:k1 + 1, :] * xr_ref[...]
    for j in range(k1):
        off = pad - (k1 - j) * S
        xc = xc + cw_ref[j:j + 1, :] * xcat[off:off + rows, :]
    tail = xcat[pad + rows - k1 * S:pad + rows, :]
    xcat[pad - k1 * S:pad, :] = tail
    nc_ref[...] = tail

    lam = lam_ref[...]
    softplus_neg_lam = jnp.maximum(-lam, 0.0) + jnp.log1p(jnp.exp(-jnp.abs(lam)))
    xcb = xc.astype(bf16)
    for n in range(RNN_BLOCKS):
        sl = slice(n * RNN_BLK, (n + 1) * RNN_BLK)
        gates = jnp.dot(xcb[:, sl], wg_ref[n], preferred_element_type=f32)
        r = jax.nn.sigmoid(gates[:, :RNN_BLK] + ba_ref[:, sl])
        i = jax.nn.sigmoid(gates[:, RNN_BLK:] + bx_ref[:, sl])
        log_a = (-LRU_C) * r * softplus_neg_lam[:, sl]
        a = jnp.exp(log_a)
        one_minus_a2 = -jnp.tanh(log_a) * (a * a + 1.0)
        sa[0, ipad:ipad + rows, sl] = a
        su[0, ipad:ipad + rows, sl] = jnp.sqrt(one_minus_a2) * (i * xc[:, sl])

    n_rounds = Tt.bit_length() - 1
    for rnd in range(n_rounds):
        d = (1 << rnd) * S
        src, dst = rnd % 2, 1 - rnd % 2
        a = sa[src, ipad:ipad + rows, :]
        su[dst, ipad:ipad + rows, :] = a * su[src, ipad - d:ipad - d + rows, :] + su[src, ipad:ipad + rows, :]
        sa[dst, ipad:ipad + rows, :] = a * sa[src, ipad - d:ipad - d + rows, :]
    fin = n_rounds % 2
    a_cum = sa[fin, ipad:ipad + rows, :]
    u_cum = su[fin, ipad:ipad + rows, :]
    if S == 1:
        h = a_cum * hc[...] + u_cum
    else:
        h = (a_cum.reshape(Tt, S, D_RNN) * hc[...][None] + u_cum.reshape(Tt, S, D_RNN)).reshape(rows, D_RNN)
    y_ref[...] = h * _gelu(gr_ref[...])
    su[fin, ipad:ipad + rows, :] = h
    last = su[fin, ipad + rows - S:ipad + rows, :]
    hc[...] = last
    hl_ref[...] = last


def _rglru(z3, prev, h0, cw, cb, wg, ba, bx, lam, *, S, Tt):
    nb, rper, _ = z3.shape
    rows = Tt * S
    assert Tt & (Tt - 1) == 0 and rper % rows == 0 and Tt >= RNN_CONV - 1
    k1 = RNN_CONV - 1
    pad = _round_up(k1 * S, SUBLANES)
    ipad = max((Tt // 2) * S, SUBLANES)
    vec = lambda: pl.BlockSpec((1, D_RNN), lambda b, t: (0, 0))
    return pl.pallas_call(
        functools.partial(_rglru_kernel, S=S, Tt=Tt),
        grid=(nb, rper // rows),
        in_specs=[
            pl.BlockSpec((None, rows, D_RNN), lambda b, t: (b, t, COL_XR // D_RNN)),
            pl.BlockSpec((None, rows, D_RNN), lambda b, t: (b, t, COL_GR // D_RNN)),
            pl.BlockSpec((None, k1 * S, D_RNN), lambda b, t: (b, 0, 0)),
            pl.BlockSpec((None, S, D_RNN), lambda b, t: (b, 0, 0)),
            pl.BlockSpec((RNN_CONV, D_RNN), lambda b, t: (0, 0)),
            vec(),
            pl.BlockSpec((RNN_BLOCKS, RNN_BLK, 2 * RNN_BLK), lambda b, t: (0, 0, 0)),
            vec(), vec(), vec(),
        ],
        out_specs=[
            pl.BlockSpec((None, rows, D_RNN), lambda b, t: (b, t, 0)),
            pl.BlockSpec((None, S, D_RNN), lambda b, t: (b, 0, 0)),
            pl.BlockSpec((None, k1 * S, D_RNN), lambda b, t: (b, 0, 0)),
        ],
        out_shape=[
            jax.ShapeDtypeStruct((nb, rper, D_RNN), f32),
            jax.ShapeDtypeStruct((nb, S, D_RNN), f32),
            jax.ShapeDtypeStruct((nb, k1 * S, D_RNN), f32),
        ],
        scratch_shapes=[
            pltpu.VMEM((pad + rows, D_RNN), f32),
            pltpu.VMEM((S, D_RNN), f32),
            pltpu.VMEM((2, ipad + rows, D_RNN), f32),
            pltpu.VMEM((2, ipad + rows, D_RNN), f32),
        ],
        compiler_params=_params("parallel", "arbitrary"),
        name="rglru",
    )(z3, z3, prev, h0, cw, cb, wg, ba, bx, lam)


def _kc_kernel(*refs, n_pieces):
    refs = refs[len(refs) - (n_pieces + 5):]
    x_refs = refs[:n_pieces]
    wc_ref, pw_ref, pos_ref, kc_ref, xp = refs[n_pieces:]
    per_tok = 2 * N_KV
    ch = CMP_STRIDE * per_tok
    nc = 0
    for r in x_refs:
        for k in range(r.shape[0] // ch):
            xp[nc * KC_PITCH:nc * KC_PITCH + ch, :] = r[k * ch:(k + 1) * ch, :]
            nc += 1
    for c in range(2):
        groups = [jnp.concatenate([xp[pl.ds(l * per_tok + c * N_KV + g, nc, stride=KC_PITCH), :]
                                   for l in range(CMP_STRIDE)], axis=1) for g in range(N_KV)]
        xs = jnp.concatenate(groups, axis=0).astype(bf16)
        fs = jnp.dot(xs, wc_ref[c], preferred_element_type=f32)
        pos = jnp.broadcast_to(pos_ref[c], (SUBLANES, CMP_BLOCK * HEAD_DIM)).astype(bf16)
        pos_bias = jnp.dot(pos, pw_ref[c], preferred_element_type=f32)[0:1, :]
        for g in range(N_KV):
            first = fs[g * nc:(g + 1) * nc, :HEAD_DIM]
            second = fs[g * nc:(g + 1) * nc, HEAD_DIM:]
            kc_ref[c * N_KV + g] = first + pltpu.roll(second, nc - 1, 0) + pos_bias


def _kc_call(x_arrays, x_specs, nseq, nc, wc, pw, pos, page_table=None):
    n_pieces = len(x_arrays)
    nsp = 0 if page_table is None else 1
    const = lambda shape: pl.BlockSpec(shape, lambda b, *_: (0,) * len(shape))
    grid_spec = pltpu.PrefetchScalarGridSpec(
        num_scalar_prefetch=nsp,
        grid=(nseq,),
        in_specs=list(x_specs) + [
            const((2, CMP_STRIDE * HEAD_DIM, 2 * HEAD_DIM)),
            const((2, CMP_BLOCK * HEAD_DIM, HEAD_DIM)),
            const((2, 1, CMP_BLOCK * HEAD_DIM)),
        ],
        out_specs=pl.BlockSpec((None, 2 * N_KV, nc, HEAD_DIM), lambda b, *_: (b, 0, 0, 0)),
        scratch_shapes=[pltpu.VMEM((nc * KC_PITCH, HEAD_DIM), f32)],
    )
    args = ([] if page_table is None else [page_table]) + list(x_arrays) + [wc, pw, pos]
    return pl.pallas_call(
        functools.partial(_kc_kernel, n_pieces=n_pieces),
        grid_spec=grid_spec,
        out_shape=jax.ShapeDtypeStruct((nseq, 2 * N_KV, nc, HEAD_DIM), f32),
        compiler_params=_params("parallel"),
        name="kc_sample" if nsp else "kc_prompt",
    )(*args)


def _pos_cols(n_rows, step):
    r = lax.broadcasted_iota(jnp.int32, (n_rows, HEAD_DIM), 0) * step
    lane = lax.broadcasted_iota(jnp.int32, (n_rows, HEAD_DIM), 1)
    low_bits = SLC_BLOCK.bit_length() - 1
    hi = (r >> low_bits) << low_bits
    return jnp.where(lane == 0, hi, jnp.where(lane == 1, r - hi, 0)).astype(f32).astype(bf16)


def _keys_with_pos(k, step=1):
    return jnp.concatenate([k.astype(bf16), _pos_cols(k.shape[0], step)], axis=1)


def _ones_col(n_rows):
    lane = lax.broadcasted_iota(jnp.int32, (n_rows, HEAD_DIM), 1)
    return jnp.where(lane == 0, 1.0, 0.0).astype(bf16)


def _vals_with_ones(v):
    return jnp.concatenate([v.astype(bf16), _ones_col(v.shape[0])], axis=1)


def _exp_rows(s, mask_bias):
    s = s + mask_bias
    m = jnp.max(s, axis=-1, keepdims=True)
    e = jnp.exp(s - m)
    inv = jnp.where(m > 0.5 * NEG, 1.0 / jnp.maximum(jnp.sum(e, axis=-1, keepdims=True), 1e-30), 0.0)
    return e, inv


def _select_blocks(psum, qpos, tb):
    lanes_per_blk = SLC_BLOCK // CMP_STRIDE
    n_lane = psum.shape[1]
    imp = psum + pltpu.roll(psum, 1, 1)
    for back in range(1, lanes_per_blk):
        imp = imp + pltpu.roll(psum, n_lane - back, 1)
    lane = lax.broadcasted_iota(jnp.int32, (tb, n_lane), 1)
    lane_f = lane.astype(f32)
    blk = lane >> (lanes_per_blk.bit_length() - 1)
    cur = qpos >> (SLC_BLOCK.bit_length() - 1)
    cand = ((lane & (lanes_per_blk - 1)) == 0) & (blk >= 1) & (blk < cur)
    score = jnp.where(cand, imp, NEG)
    sel = (lane == 0) & (cur > 0)
    for _ in range(TOPK - 2):
        best = jnp.max(score, axis=-1, keepdims=True)
        first = jnp.min(jnp.where(score == best, lane_f, float(n_lane)), axis=-1, keepdims=True)
        pick = (lane_f == first) & (best > 0.5 * NEG)
        sel = sel | pick
        score = jnp.where(pick, NEG, score)
    return jnp.where(sel, 1.0, 0.0).astype(bf16)


def _attn_core(q, qpos, sg, kc_ref, sk_ref, sv_ref, wk_ref, wv_ref, *, tb, slc_rows, win_row0, win_pos0):
    scale = 1.0 / math.sqrt(HEAD_DIM)
    n_cmp = kc_ref.shape[1]
    slc_shift = SLC_BLOCK.bit_length() - 1
    cur = qpos >> slc_shift
    lane_c = lax.broadcasted_iota(jnp.int32, (tb, n_cmp), 1)
    bias_c = jnp.where(qpos - (CMP_STRIDE * lane_c + (CMP_BLOCK - 1)) >= 0, 0.0, NEG)
    lane_q = lax.broadcasted_iota(jnp.int32, (tb, HEAD_DIM), 1)
    hs = lambda x, p: x[p * tb:(p + 1) * tb]
    outs = []
    for g in range(N_KV):
        parts = []
        for p in range(HPG):
            h = g * HPG + p
            slope = 2.0 ** (-8.0 * (h + 1) / N_HEADS)
            parts.append(jnp.concatenate([q[:, h * HEAD_DIM:(h + 1) * HEAD_DIM] * scale,
                                          jnp.where(lane_q < 2, slope, 0.0)], axis=1))
        qg = jnp.concatenate(parts, axis=0).astype(bf16)

        s = _nt(qg, _keys_with_pos(kc_ref[g], CMP_STRIDE))
        ps = []
        for p in range(HPG):
            e, inv = _exp_rows(hs(s, p), bias_c)
            ps.append(e * inv)
        o_cmp = jnp.dot(jnp.concatenate(ps, axis=0).astype(bf16), kc_ref[N_KV + g].astype(bf16),
                        preferred_element_type=f32)
        selm = _select_blocks(ps[0] + ps[1] + ps[2] + ps[3], qpos, tb)

        def attend(keys, vals, bias):
            s = _nt(qg, keys)
            es = []
            for p in range(HPG):
                sp = hs(s, p) + bias
                es.append(jnp.exp(sp - jnp.max(sp, axis=-1, keepdims=True)))
            pv = jnp.dot(jnp.concatenate(es, axis=0).astype(bf16), vals, preferred_element_type=f32)
            return pv[:, :HEAD_DIM] / jnp.maximum(pv[:, HEAD_DIM:HEAD_DIM + 1], 1e-30)

        kpos = lax.broadcasted_iota(jnp.int32, (1, slc_rows), 1)
        kblk = kpos >> slc_shift
        hot = lax.broadcasted_iota(jnp.int32, (n_cmp, slc_rows), 0) == kblk * (SLC_BLOCK // CMP_STRIDE)
        chosen = jnp.dot(selm, jnp.where(hot, 1.0, 0.0).astype(bf16), preferred_element_type=f32)
        bias_s = jnp.where((chosen > 0.5) | ((kblk == cur) & (qpos - kpos >= 0)), 0.0, NEG)
        o_slc = attend(sk_ref[g, 0:slc_rows, :], sv_ref[g, 0:slc_rows, :], bias_s)

        kpos = win_pos0 + lax.broadcasted_iota(jnp.int32, (1, WIN_KEYS), 1)
        d = qpos - kpos
        bias_w = jnp.where((d >= 0) & (d <= WINDOW) & (kpos >= 0), 0.0, NEG)
        o_win = attend(wk_ref[g, pl.ds(win_row0, WIN_KEYS), :], wv_ref[g, pl.ds(win_row0, WIN_KEYS), :], bias_w)

        for p in range(HPG):
            h = g * HPG + p
            outs.append(sg[:, h:h + 1] * hs(o_cmp, p) + sg[:, N_HEADS + h:N_HEADS + h + 1] * hs(o_slc, p)
                        + sg[:, 2 * N_HEADS + h:2 * N_HEADS + h + 1] * hs(o_win, p))
    return jnp.concatenate(outs, axis=1)


def _attn_scratch(slc_rows, win_rows):
    return [pltpu.VMEM((N_KV, rows, 2 * HEAD_DIM), bf16) for rows in (slc_rows, slc_rows, win_rows, win_rows)]


def _attn_prompt_kernel(q_ref, kvs_ref, kvw_ref, gn_ref, kc_ref, o_ref, sk, sv, wk, wv, *, T, sb):
    tb = WIN_Q_BLOCK
    i = sb * (SLC_CHUNK // tb) + pl.program_id(1)

    @pl.when(pl.program_id(1) == 0)
    def _():
        for g in range(N_KV):
            for src, kdst, vdst in ((kvs_ref, sk, sv), (kvw_ref, wk, wv)):
                kdst[g] = _keys_with_pos(src[:, g * HEAD_DIM:(g + 1) * HEAD_DIM])
                vdst[g] = _vals_with_ones(src[:, (N_KV + g) * HEAD_DIM:(N_KV + g + 1) * HEAD_DIM])

    qpos = i * tb + lax.broadcasted_iota(jnp.int32, (tb, 1), 0)
    win0 = pl.multiple_of(jnp.clip(i * tb - WINDOW, 0, T - WIN_KEYS), LANES)
    o_ref[...] = _attn_core(
        q_ref[...], qpos, jax.nn.sigmoid(gn_ref[:, 0:LANES]), kc_ref, sk, sv, wk, wv, tb=tb,
        slc_rows=(sb + 1) * SLC_CHUNK, win_row0=win0, win_pos0=win0)


def _attn_prompt(z3, kc, sb):
    B, T, _ = z3.shape
    assert T % SLC_CHUNK == 0 and T >= WIN_KEYS
    tb = WIN_Q_BLOCK
    nq = SLC_CHUNK // tb
    return pl.pallas_call(
        functools.partial(_attn_prompt_kernel, T=T, sb=sb),
        grid=(B, nq),
        in_specs=[
            pl.BlockSpec((None, tb, D_ATT), lambda b, i: (b, sb * nq + i, COL_Q // D_ATT)),
            pl.BlockSpec((None, T, KV_W), lambda b, i: (b, 0, COL_KVS // KV_W)),
            pl.BlockSpec((None, T, KV_W), lambda b, i: (b, 0, COL_KVW // KV_W)),
            pl.BlockSpec((None, tb, GN_PAD), lambda b, i: (b, sb * nq + i, COL_GN // GN_PAD)),
            pl.BlockSpec((None, 2 * N_KV, T // CMP_STRIDE, HEAD_DIM), lambda b, i: (b, 0, 0, 0)),
        ],
        out_specs=pl.BlockSpec((None, tb, D_ATT), lambda b, i: (b, i, 0)),
        out_shape=jax.ShapeDtypeStruct((B, SLC_CHUNK, D_ATT), f32),
        scratch_shapes=_attn_scratch(T, T),
        compiler_params=_params("parallel", "arbitrary"),
        name=f"attn_prompt_{sb}",
    )(z3, z3, z3, z3, kc)


def _attn_sample_kernel(*refs, n_seq, n_pages, page, tq, tb):
    zq_ref, kc_ref, win_ref = refs[1:4]
    page_refs = refs[4:4 + n_seq * n_pages]
    o_ref, skw_ref = refs[4 + n_seq * n_pages:6 + n_seq * n_pages]
    scratch = refs[6 + n_seq * n_pages:]
    for s in range(n_seq):
        _attn_sample_one(zq_ref.at[s], kc_ref.at[s], win_ref.at[s], page_refs[s * n_pages:(s + 1) * n_pages],
                         o_ref.at[s], skw_ref.at[s], *scratch[4 * s:4 * s + 4], page=page, tq=tq, tb=tb)


def _attn_sample_one(zq_ref, kc_ref, win_ref, page_refs, o_ref, skw_ref, sk, sv, wk, wv, *, page, tq, tb):
    n_pages = len(page_refs)
    per_tok = 2 * N_KV
    past = n_pages * page
    wbuf = win_ref.shape[0] // per_tok
    q0 = 0
    c_kvs = q0 + D_ATT + KV_W
    c_kvw = c_kvs + KV_W
    c_gn = c_kvw + KV_W

    def new_rows(col):
        return jnp.concatenate([zq_ref[:, col:col + HEAD_DIM], jnp.zeros((LANES - tb, HEAD_DIM), f32)],
                               axis=0).astype(bf16)

    for g in range(N_KV):
        for j in range(n_pages):
            rows = slice(j * page, (j + 1) * page)
            sk[g, rows, 0:HEAD_DIM] = page_refs[j][pl.ds(g, page, stride=per_tok), :].astype(bf16)
            sv[g, rows, 0:HEAD_DIM] = page_refs[j][pl.ds(N_KV + g, page, stride=per_tok), :].astype(bf16)
        sk[g, past:past + LANES, 0:HEAD_DIM] = new_rows(c_kvs + g * HEAD_DIM)
        sv[g, past:past + LANES, 0:HEAD_DIM] = new_rows(c_kvs + (N_KV + g) * HEAD_DIM)
        sk[g, :, HEAD_DIM:] = _pos_cols(past + LANES, 1)
        sv[g, :, HEAD_DIM:] = _ones_col(past + LANES)
        wk[g, 0:wbuf, 0:HEAD_DIM] = win_ref[pl.ds(g, wbuf, stride=per_tok), :].astype(bf16)
        wv[g, 0:wbuf, 0:HEAD_DIM] = win_ref[pl.ds(N_KV + g, wbuf, stride=per_tok), :].astype(bf16)
        wk[g, wbuf:wbuf + LANES, 0:HEAD_DIM] = new_rows(c_kvw + g * HEAD_DIM)
        wv[g, wbuf:wbuf + LANES, 0:HEAD_DIM] = new_rows(c_kvw + (N_KV + g) * HEAD_DIM)
        wk[g, :, HEAD_DIM:] = _pos_cols(wbuf + LANES, 1)
        wv[g, :, HEAD_DIM:] = _ones_col(wbuf + LANES)
    skw_ref[0:(wbuf - tq) * per_tok, :] = win_ref[tq * per_tok:wbuf * per_tok, :]
    for t in range(tq):
        for cg in range(per_tok):
            r = (wbuf - tq + t) * per_tok + cg
            skw_ref[r:r + 1, :] = zq_ref[t:t + 1, c_kvw + cg * HEAD_DIM:c_kvw + (cg + 1) * HEAD_DIM]

    qpos = past + lax.broadcasted_iota(jnp.int32, (tb, 1), 0)
    o_ref[...] = _attn_core(
        zq_ref[:, q0:q0 + D_ATT], qpos, jax.nn.sigmoid(zq_ref[:, c_gn:c_gn + LANES]), kc_ref, sk, sv, wk, wv,
        tb=tb, slc_rows=past + LANES, win_row0=0, win_pos0=past - wbuf)


def _attn_sample(zq, kc, cache_win, cache_slc, page_table, *, tq):
    DB, tb, zw = zq.shape
    per_tok = 2 * N_KV
    n_pages = page_table.shape[1]
    page = cache_slc.shape[1] // per_tok
    past = n_pages * page
    wbuf = cache_win.shape[1] // per_tok
    n_seq = 1
    assert wbuf + LANES == WIN_KEYS and tq <= tb and tq < CMP_STRIDE and (tq * per_tok) % SUBLANES == 0
    page_specs = [pl.BlockSpec((None, page * per_tok, HEAD_DIM),
                               functools.partial(lambda b, pt, s, j: (pt[b * n_seq + s, j], 0, 0), s=s, j=j))
                  for s in range(n_seq) for j in range(n_pages)]
    grid_spec = pltpu.PrefetchScalarGridSpec(
        num_scalar_prefetch=1,
        grid=(DB // n_seq,),
        in_specs=[
            pl.BlockSpec((n_seq, tb, zw), lambda b, pt: (b, 0, 0)),
            pl.BlockSpec((n_seq, 2 * N_KV, past // CMP_STRIDE, HEAD_DIM), lambda b, pt: (b, 0, 0, 0)),
            pl.BlockSpec((n_seq, wbuf * per_tok, HEAD_DIM), lambda b, pt: (b, 0, 0)),
        ] + page_specs,
        out_specs=[
            pl.BlockSpec((n_seq, tb, D_ATT), lambda b, pt: (b, 0, 0)),
            pl.BlockSpec((n_seq, wbuf * per_tok, HEAD_DIM), lambda b, pt: (b, 0, 0)),
        ],
        scratch_shapes=_attn_scratch(past + LANES, wbuf + LANES) * n_seq,
    )
    return pl.pallas_call(
        functools.partial(_attn_sample_kernel, n_seq=n_seq, n_pages=n_pages, page=page, tq=tq, tb=tb),
        grid_spec=grid_spec,
        out_shape=[jax.ShapeDtypeStruct((DB, tb, D_ATT), f32),
                   jax.ShapeDtypeStruct((DB, wbuf * per_tok, HEAD_DIM), f32)],
        compiler_params=_params("parallel"),
        name="attn_sample",
    )(page_table, zq, kc, cache_win, *([cache_slc] * (n_seq * n_pages)))


def _merge_kernel(x_ref, yr_ref, o_ref, gm1_ref, gm2_ref, wr_ref, wa_ref, wo_ref, out_ref):
    m = (jax.nn.sigmoid(gm1_ref[...]) * jnp.dot(yr_ref[...].astype(bf16), wr_ref[...], preferred_element_type=f32)
         + jax.nn.sigmoid(gm2_ref[...]) * jnp.dot(o_ref[...].astype(bf16), wa_ref[...], preferred_element_type=f32))
    out_ref[...] = x_ref[...] + jnp.dot(m.astype(bf16), wo_ref[...], preferred_element_type=f32)


def _merge(x2d, yr, o, z2d, wr, wa, wo):
    rows = x2d.shape[0]
    tm = min(256, rows)
    const = lambda shape: pl.BlockSpec(shape, lambda i: (0, 0), pipeline_mode=pl.Buffered(1))
    return pl.pallas_call(
        _merge_kernel,
        grid=(rows // tm,),
        in_specs=[
            pl.BlockSpec((tm, D_MODEL), lambda i: (i, 0)),
            pl.BlockSpec((tm, D_RNN), lambda i: (i, 0)),
            pl.BlockSpec((tm, D_ATT), lambda i: (i, 0)),
            pl.BlockSpec((tm, D_MODEL), lambda i: (i, COL_GM // D_MODEL)),
            pl.BlockSpec((tm, D_MODEL), lambda i: (i, COL_GM // D_MODEL + 1)),
            const((D_RNN, D_MODEL)), const((D_ATT, D_MODEL)), const((D_MODEL, D_MODEL)),
        ],
        out_specs=pl.BlockSpec((tm, D_MODEL), lambda i: (i, 0)),
        out_shape=jax.ShapeDtypeStruct((rows, D_MODEL), f32),
        compiler_params=_params("parallel"),
        name="merge",
    )(x2d, yr, o, z2d, z2d, wr, wa, wo)


def _ffn_kernel(x_ref, prev_ref, gf_ref, wg_ref, wu_ref, wd_ref, cw_ref, cb_ref, gl_ref,
                y_ref, fc_ref, h_ref, acc_ref, gcat, carry, *, S, rows, final_norm):
    ti, fi = pl.program_id(1), pl.program_id(2)
    k1 = FFN_CONV - 1
    pad = _round_up(k1 * S, SUBLANES)

    @pl.when(fi == 0)
    def _():
        h_ref[...] = _rms(x_ref[...], gf_ref[...]).astype(bf16)
        acc_ref[...] = jnp.zeros_like(acc_ref)

    @pl.when(ti == 0)
    def _():
        carry[fi, pad - k1 * S:pad, :] = prev_ref[...]

    h = h_ref[...]
    gate = jnp.dot(h, wg_ref[...], preferred_element_type=f32)
    gcat[pad - k1 * S:pad, :] = carry[fi, pad - k1 * S:pad, :]
    gcat[pad:pad + rows, :] = gate
    u = cb_ref[...] + cw_ref[k1:k1 + 1, :] * gate
    for j in range(k1):
        off = pad - (k1 - j) * S
        u = u + cw_ref[j:j + 1, :] * gcat[off:off + rows, :]
    tail = gcat[pad + rows - k1 * S:pad + rows, :]
    carry[fi, pad - k1 * S:pad, :] = tail
    fc_ref[...] = tail
    up = jnp.dot(h, wu_ref[...], preferred_element_type=f32)
    acc_ref[...] += jnp.dot((_gelu(u) * up).astype(bf16), wd_ref[...], preferred_element_type=f32)

    @pl.when(fi == pl.num_programs(2) - 1)
    def _():
        out = x_ref[...] + acc_ref[...]
        y_ref[...] = _rms(out, gl_ref[...]) if final_norm else out


def _ffn(x3, prev, g_ffn, wg, wu, wd, cw, cb, g_fin, *, S, Tt, final_norm):
    nb, rper, _ = x3.shape
    rows = Tt * S
    tf = 768
    nf = D_FF // tf
    k1 = FFN_CONV - 1
    pad = _round_up(k1 * S, SUBLANES)
    assert rper % rows == 0 and Tt >= k1
    return pl.pallas_call(
        functools.partial(_ffn_kernel, S=S, rows=rows, final_norm=final_norm),
        grid=(nb, rper // rows, nf),
        in_specs=[
            pl.BlockSpec((None, rows, D_MODEL), lambda b, t, f: (b, t, 0)),
            pl.BlockSpec((None, k1 * S, tf), lambda b, t, f: (b, 0, f)),
            pl.BlockSpec((1, D_MODEL), lambda b, t, f: (0, 0)),
            pl.BlockSpec((D_MODEL, tf), lambda b, t, f: (0, f)),
            pl.BlockSpec((D_MODEL, tf), lambda b, t, f: (0, f)),
            pl.BlockSpec((tf, D_MODEL), lambda b, t, f: (f, 0)),
            pl.BlockSpec((FFN_CONV, tf), lambda b, t, f: (0, f)),
            pl.BlockSpec((1, tf), lambda b, t, f: (0, f)),
            pl.BlockSpec((1, D_MODEL), lambda b, t, f: (0, 0)),
        ],
        out_specs=[
            pl.BlockSpec((None, rows, D_MODEL), lambda b, t, f: (b, t, 0)),
            pl.BlockSpec((None, None, k1 * S, tf), lambda b, t, f: (b, t, 0, f)),
        ],
        out_shape=[
            jax.ShapeDtypeStruct((nb, rper, D_MODEL), f32),
            jax.ShapeDtypeStruct((nb, rper // rows, k1 * S, D_FF), f32),
        ],
        scratch_shapes=[
            pltpu.VMEM((rows, D_MODEL), bf16),
            pltpu.VMEM((rows, D_MODEL), f32),
            pltpu.VMEM((pad + rows, tf), f32),
            pltpu.VMEM((nf, pad, tf), f32),
        ],
        compiler_params=_params("parallel", "arbitrary", "arbitrary"),
        name="ffn",
    )(x3, prev, g_ffn, wg, wu, wd, cw, cb, g_fin)


def _time_major(a):
    return jnp.swapaxes(a, 0, 1).reshape(1, a.shape[0] * a.shape[1], a.shape[2])


def _batch_major(a, db):
    return jnp.swapaxes(a.reshape(a.shape[1] // db, db, a.shape[2]), 0, 1)


def kernel(x_prompt, x_sample, cache_kv_cmp, cache_kv_slc, cache_kv_win, state_rnn_h, state_rnn_conv, state_ffn_conv, page_table, norm_mix, w_in, rnn_conv_w, rnn_conv_b, rnn_wa, rnn_ba, rnn_wx, rnn_bx, rnn_lambda, cmp_w, cmp_pos, w_proj_rnn, w_proj_att, w_out, norm_ffn, ffn_w_gate, ffn_w_up, ffn_conv_w, ffn_conv_b, ffn_w_down, norm_final):
    B, T, _ = x_prompt.shape
    DB, Tq, _ = x_sample.shape
    depth = w_in.shape[0]
    n_pool, page = cache_kv_cmp.shape[1:3]
    n_pages = page_table.shape[1]
    past = n_pages * page
    wbuf = cache_kv_win.shape[2]
    tb = SUBLANES
    assert page % CMP_STRIDE == 0 and page % SLC_BLOCK == 0 and Tq >= RNN_CONV - 1 and Tq <= tb
    assert Tq & (Tq - 1) == 0 and DB % SUBLANES == 0 and T % 256 == 0
    kv_shape = (2, N_KV, HEAD_DIM)

    xp = x_prompt.reshape(B * T, D_MODEL)
    xs = _time_major(x_sample)[0]
    pst = [[] for _ in range(6)]
    sst = [[] for _ in range(6)]
    g_out = norm_final.reshape(1, D_MODEL)
    for l in range(depth):
        w = w_in[l]
        widths = (D_RNN, D_RNN, D_ATT, KV_W, KV_W, KV_W, 3 * N_HEADS, 2 * D_MODEL)
        cuts = [0]
        for wd_ in widths:
            cuts.append(cuts[-1] + wd_)
        part = lambda k: w[:, cuts[k]:cuts[k + 1]]
        w_cat = jnp.concatenate(
            [part(0), part(1), part(7), part(2), part(3), part(4), part(5),
             jnp.pad(part(6), ((0, 0), (0, GN_PAD - 3 * N_HEADS)))], axis=1).astype(bf16)
        g_mix = norm_mix[l].reshape(1, D_MODEL)
        cw, cb = rnn_conv_w[l], rnn_conv_b[l].reshape(1, D_RNN)
        wgate = jnp.concatenate([rnn_wa[l], rnn_wx[l]], axis=-1).astype(bf16)
        ba, bx, lam = (v[l].reshape(1, D_RNN) for v in (rnn_ba, rnn_bx, rnn_lambda))
        cmpw = cmp_w[l].astype(bf16)
        wc = jnp.concatenate([cmpw[:, :CMP_STRIDE].reshape(2, CMP_STRIDE * HEAD_DIM, HEAD_DIM),
                              cmpw[:, CMP_STRIDE:].reshape(2, CMP_STRIDE * HEAD_DIM, HEAD_DIM)], axis=-1)
        pw = cmpw.reshape(2, CMP_BLOCK * HEAD_DIM, HEAD_DIM)
        pos = cmp_pos[l].reshape(2, 1, CMP_BLOCK * HEAD_DIM)
        wr, wa, wo = w_proj_rnn[l].astype(bf16), w_proj_att[l].astype(bf16), w_out[l].astype(bf16)
        g_ffn = norm_ffn[l].reshape(1, D_MODEL)
        fwg, fwu, fwd = ffn_w_gate[l].astype(bf16), ffn_w_up[l].astype(bf16), ffn_w_down[l].astype(bf16)
        fcw, fcb = ffn_conv_w[l], ffn_conv_b[l].reshape(1, D_FF)
        last = l == depth - 1

        z = _inproj(xp, g_mix, w_cat)
        z3 = z.reshape(B, T, D_Z)
        kvc, kvs, kvw = (z3[:, :, c:c + KV_W] for c in (COL_KVC, COL_KVS, COL_KVW))
        y_rnn, h_last, rconv = _rglru(z3, jnp.zeros((B, RNN_CONV - 1, D_RNN), f32), jnp.zeros((B, 1, D_RNN), f32),
                                      cw, cb, wgate, ba, bx, lam, S=1, Tt=256)
        per_tok = 2 * N_KV
        kvc_out = kvc.reshape(B, T, *kv_shape)
        kc = _kc_call([kvc_out.reshape(B, T * per_tok, HEAD_DIM)],
                      [pl.BlockSpec((None, T * per_tok, HEAD_DIM), lambda b: (b, 0, 0))],
                      B, T // CMP_STRIDE, wc, pw, pos)
        o = jnp.concatenate([_attn_prompt(z3, kc, sb) for sb in range(T // SLC_CHUNK)], axis=1)
        x1 = _merge(xp, y_rnn.reshape(B * T, D_RNN), o.reshape(B * T, D_ATT), z, wr, wa, wo)
        y, fconv = _ffn(x1.reshape(B, T, D_MODEL), jnp.zeros((B, FFN_CONV - 1, D_FF), f32), g_ffn, fwg, fwu, fwd,
                        fcw, fcb, g_out, S=1, Tt=512, final_norm=last)
        xp = y.reshape(B * T, D_MODEL)
        wlen = min(WINDOW, T)
        for lst, v in zip(pst, (kvc_out, kvs.reshape(B, T, *kv_shape),
                                kvw[:, T - wlen:].reshape(B, wlen, *kv_shape), h_last[:, 0], rconv, fconv[:, -1])):
            lst.append(v)

        zs = _inproj(xs, g_mix, w_cat)
        zs3 = zs.reshape(1, Tq * DB, D_Z)
        ys_rnn, hs_last, sconv = _rglru(zs3, _time_major(state_rnn_conv[l]), state_rnn_h[l][None],
                                        cw, cb, wgate, ba, bx, lam, S=DB, Tt=Tq)
        zq = _batch_major(zs3[:, :, COL_Q:], DB)
        zq = jnp.pad(zq, ((0, 0), (0, tb - Tq), (0, 0)))
        kcs = _kc_call(
            [cache_kv_cmp[l].reshape(n_pool, page * per_tok, HEAD_DIM)] * n_pages,
            [pl.BlockSpec((None, page * per_tok, HEAD_DIM),
                          functools.partial(lambda b, pt, j: (pt[b, j], 0, 0), j=j)) for j in range(n_pages)],
            DB, past // CMP_STRIDE, wc, pw, pos, page_table=page_table)
        os_, s_win = _attn_sample(zq, kcs, cache_kv_win[l].reshape(DB, wbuf * per_tok, HEAD_DIM),
                                  cache_kv_slc[l].reshape(n_pool, page * per_tok, HEAD_DIM), page_table, tq=Tq)
        os_tm = _time_major(os_[:, :Tq])[0]
        xs1 = _merge(xs, ys_rnn[0], os_tm, zs, wr, wa, wo)
        ys, sfconv = _ffn(xs1[None], _time_major(state_ffn_conv[l]), g_ffn, fwg, fwu, fwd, fcw, fcb, g_out,
                          S=DB, Tt=Tq, final_norm=last)
        xs = ys[0]
        c0 = D_ATT
        for lst, v in zip(sst, (zq[:, :Tq, c0:c0 + KV_W].reshape(DB, Tq, *kv_shape),
                                zq[:, :Tq, c0 + KV_W:c0 + 2 * KV_W].reshape(DB, Tq, *kv_shape),
                                s_win.reshape(DB, wbuf, *kv_shape), hs_last[0],
                                _batch_major(sconv, DB), _batch_major(sfconv[:, -1], DB))):
            lst.append(v)

    y_prompt = xp.reshape(B, T, D_MODEL)
    y_sample = _batch_major(xs[None], DB)
    return (y_prompt, y_sample, *[jnp.stack(v) for v in pst], *[jnp.stack(v) for v in sst])
```

```python
import functools
import math

import jax
import jax.numpy as jnp
from jax import lax
from jax.experimental import pallas as pl
from jax.experimental.pallas import tpu as pltpu

D_MODEL = 2048
D_RNN = D_MODEL // 2
RNN_BLOCKS = 8
RNN_BLK = D_RNN // RNN_BLOCKS
RNN_CONV = 4
LRU_C = 8.0
HEAD_DIM = 128
D_ATT = D_MODEL // 2
N_HEADS = D_ATT // HEAD_DIM
N_KV = 2
HPG = N_HEADS // N_KV
KV_W = 2 * N_KV * HEAD_DIM
CMP_BLOCK = 32
CMP_STRIDE = 16
SLC_BLOCK = 64
TOPK = 8
WINDOW = 512
WIN_Q_BLOCK = 128
D_FF = 3 * D_MODEL
FFN_CONV = 3
RMS_EPS = 1e-6
NEG = -1e30

COL_XR = 0
COL_GR = COL_XR + D_RNN
COL_GM = COL_GR + D_RNN
COL_Q = COL_GM + 2 * D_MODEL
COL_KVC = COL_Q + D_ATT
COL_KVS = COL_KVC + KV_W
COL_KVW = COL_KVS + KV_W
COL_GN = COL_KVW + KV_W
GN_PAD = KV_W
D_Z = COL_GN + GN_PAD

VMEM_LIMIT_BYTES = 56 * 1024 * 1024
SUBLANES = 8
LANES = 128

WIN_KEYS = WINDOW + WIN_Q_BLOCK
KC_PITCH = CMP_STRIDE * 2 * N_KV + SUBLANES
SLC_CHUNK = 512

f32 = jnp.float32
bf16 = jnp.bfloat16


def _round_up(x, m):
    return (x + m - 1) // m * m


def _params(*sem):
    return pltpu.CompilerParams(dimension_semantics=sem, vmem_limit_bytes=VMEM_LIMIT_BYTES)


def _rms(x, g):
    return (x * lax.rsqrt(jnp.mean(x * x, axis=-1, keepdims=True) + RMS_EPS)) * g


def _gelu(x):
    return x * (0.5 * (1.0 + jnp.tanh(math.sqrt(2.0 / math.pi) * (x + 0.044715 * (x * x * x)))))


def _nt(a, b):
    return lax.dot_general(a, b, (((1,), (1,)), ((), ())), preferred_element_type=f32)


def _inproj_kernel(x_ref, g_ref, w_ref, o_ref, *refs, tm, tn):
    kv_refs, h_ref = refs[:3], refs[3]

    @pl.when(pl.program_id(1) == 0)
    def _():
        h_ref[...] = _rms(x_ref[...], g_ref[...]).astype(bf16)

    o_ref[...] = jnp.dot(h_ref[...], w_ref[...], preferred_element_type=f32)

    per_tok = 2 * N_KV
    for col, kv_ref in zip((COL_KVC, COL_KVS, COL_KVW), kv_refs):
        lo = col % tn

        @pl.when(pl.program_id(1) == col // tn)
        def _(lo=lo, kv_ref=kv_ref):
            for cg in range(per_tok):
                kv_ref[pl.ds(cg, tm, stride=per_tok), :] = o_ref[:, lo + cg * HEAD_DIM:lo + (cg + 1) * HEAD_DIM]


def _inproj(x2d, g, w_cat):
    rows = x2d.shape[0]
    tm = min(1024, rows)
    tn = 1024
    per_tok = 2 * N_KV
    assert rows % tm == 0 and D_Z % tn == 0
    assert all(c % tn + KV_W <= tn for c in (COL_KVC, COL_KVS, COL_KVW))
    kv_spec = pl.BlockSpec((tm * per_tok, HEAD_DIM), lambda i, j: (i, 0))
    kv_shape = jax.ShapeDtypeStruct((rows * per_tok, HEAD_DIM), f32)
    return pl.pallas_call(
        functools.partial(_inproj_kernel, tm=tm, tn=tn),
        grid=(rows // tm, D_Z // tn),
        in_specs=[
            pl.BlockSpec((tm, D_MODEL), lambda i, j: (i, 0)),
            pl.BlockSpec((1, D_MODEL), lambda i, j: (0, 0)),
            pl.BlockSpec((D_MODEL, tn), lambda i, j: (0, j)),
        ],
        out_specs=[pl.BlockSpec((tm, tn), lambda i, j: (i, j)), kv_spec, kv_spec, kv_spec],
        out_shape=[jax.ShapeDtypeStruct((rows, D_Z), f32), kv_shape, kv_shape, kv_shape],
        scratch_shapes=[pltpu.VMEM((tm, D_MODEL), bf16)],
        compiler_params=_params("parallel", "arbitrary"),
        name="inproj",
    )(x2d, g, w_cat)


def _rglru_kernel(xr_ref, gr_ref, prev_ref, h0_ref, cw_ref, cb_ref, wg_ref, ba_ref, bx_ref, lam_ref,
                  y_ref, hl_ref, nc_ref, xcat, hc, sa, su, *, S, Tt):
    rows = Tt * S
    k1 = RNN_CONV - 1
    pad = _round_up(k1 * S, SUBLANES)
    ipad = max((Tt // 2) * S, SUBLANES)

    @pl.when(pl.program_id(1) == 0)
    def _():
        xcat[pad - k1 * S:pad, :] = prev_ref[...]
        hc[...] = h0_ref[...]
        for k in range(2):
            sa[k, 0:ipad, :] = jnp.ones((ipad, D_RNN), f32)
            su[k, 0:ipad, :] = jnp.zeros((ipad, D_RNN), f32)

    xcat[pad:pad + rows, :] = xr_ref[...]
    xc = cb_ref[...] + cw_ref[k1:k1 + 1, :] * xr_ref[...]
    for j in range(k1):
        off = pad - (k1 - j) * S
        xc = xc + cw_ref[j:j + 1, :] * xcat[off:off + rows, :]
    tail = xcat[pad + rows - k1 * S:pad + rows, :]
    xcat[pad - k1 * S:pad, :] = tail
    nc_ref[...] = tail

    lam = lam_ref[...]
    softplus_neg_lam = jnp.maximum(-lam, 0.0) + jnp.log1p(jnp.exp(-jnp.abs(lam)))
    xcb = xc.astype(bf16)
    for n in range(RNN_BLOCKS):
        sl = slice(n * RNN_BLK, (n + 1) * RNN_BLK)
        gates = jnp.dot(xcb[:, sl], wg_ref[n], preferred_element_type=f32)
        r = jax.nn.sigmoid(gates[:, :RNN_BLK] + ba_ref[:, sl])
        i = jax.nn.sigmoid(gates[:, RNN_BLK:] + bx_ref[:, sl])
        log_a = (-LRU_C) * r * softplus_neg_lam[:, sl]
        a = jnp.exp(log_a)
        one_minus_a2 = -jnp.tanh(log_a) * (a * a + 1.0)
        sa[0, ipad:ipad + rows, sl] = a
        su[0, ipad:ipad + rows, sl] = jnp.sqrt(one_minus_a2) * (i * xc[:, sl])

    n_rounds = Tt.bit_length() - 1
    for rnd in range(n_rounds):
        d = (1 << rnd) * S
        src, dst = rnd % 2, 1 - rnd % 2
        a = sa[src, ipad:ipad + rows, :]
        su[dst, ipad:ipad + rows, :] = a * su[src, ipad - d:ipad - d + rows, :] + su[src, ipad:ipad + rows, :]
        sa[dst, ipad:ipad + rows, :] = a * sa[src, ipad - d:ipad - d + rows, :]
    fin = n_rounds % 2
    a_cum = sa[fin, ipad:ipad + rows, :]
    u_cum = su[fin, ipad:ipad + rows, :]
    if S == 1:
        h = a_cum * hc[...] + u_cum
    else:
        h = (a_cum.reshape(Tt, S, D_RNN) * hc[...][None] + u_cum.reshape(Tt, S, D_RNN)).reshape(rows, D_RNN)
    y_ref[...] = h * _gelu(gr_ref[...])
    su[fin, ipad:ipad + rows, :] = h
    last = su[fin, ipad + rows - S:ipad + rows, :]
    hc[...] = last
    hl_ref[...] = last


def _rglru(z3, prev, h0, cw, cb, wg, ba, bx, lam, *, S, Tt):
    nb, rper, _ = z3.shape
    rows = Tt * S
    assert Tt & (Tt - 1) == 0 and rper % rows == 0 and Tt >= RNN_CONV - 1
    k1 = RNN_CONV - 1
    pad = _round_up(k1 * S, SUBLANES)
    ipad = max((Tt // 2) * S, SUBLANES)
    vec = lambda: pl.BlockSpec((1, D_RNN), lambda b, t: (0, 0))
    return pl.pallas_call(
        functools.partial(_rglru_kernel, S=S, Tt=Tt),
        grid=(nb, rper // rows),
        in_specs=[
            pl.BlockSpec((None, rows, D_RNN), lambda b, t: (b, t, COL_XR // D_RNN)),
            pl.BlockSpec((None, rows, D_RNN), lambda b, t: (b, t, COL_GR // D_RNN)),
            pl.BlockSpec((None, k1 * S, D_RNN), lambda b, t: (b, 0, 0)),
            pl.BlockSpec((None, S, D_RNN), lambda b, t: (b, 0, 0)),
            pl.BlockSpec((RNN_CONV, D_RNN), lambda b, t: (0, 0)),
            vec(),
            pl.BlockSpec((RNN_BLOCKS, RNN_BLK, 2 * RNN_BLK), lambda b, t: (0, 0, 0)),
            vec(), vec(), vec(),
        ],
        out_specs=[
            pl.BlockSpec((None, rows, D_RNN), lambda b, t: (b, t, 0)),
            pl.BlockSpec((None, S, D_RNN), lambda b, t: (b, 0, 0)),
            pl.BlockSpec((None, k1 * S, D_RNN), lambda b, t: (b, 0, 0)),
        ],
        out_shape=[
            jax.ShapeDtypeStruct((nb, rper, D_RNN), f32),
            jax.ShapeDtypeStruct((nb, S, D_RNN), f32),
            jax.ShapeDtypeStruct((nb, k1 * S, D_RNN), f32),
        ],
        scratch_shapes=[
            pltpu.VMEM((pad + rows, D_RNN), f32),
            pltpu.VMEM((S, D_RNN), f32),
            pltpu.VMEM((2, ipad + rows, D_RNN), f32),
            pltpu.VMEM((2, ipad + rows, D_RNN), f32),
        ],
        compiler_params=_params("parallel", "arbitrary"),
        name="rglru",
    )(z3, z3, prev, h0, cw, cb, wg, ba, bx, lam)


def _kc_kernel(*refs, n_pieces):
    refs = refs[len(refs) - (n_pieces + 5):]
    x_refs = refs[:n_pieces]
    wc_ref, pw_ref, pos_ref, kc_ref, xp = refs[n_pieces:]
    per_tok = 2 * N_KV
    ch = CMP_STRIDE * per_tok
    nc = 0
    for r in x_refs:
        for k in range(r.shape[0] // ch):
            xp[nc * KC_PITCH:nc * KC_PITCH + ch, :] = r[k * ch:(k + 1) * ch, :]
            nc += 1
    for c in range(2):
        groups = [jnp.concatenate([xp[pl.ds(l * per_tok + c * N_KV + g, nc, stride=KC_PITCH), :]
                                   for l in range(CMP_STRIDE)], axis=1) for g in range(N_KV)]
        xs = jnp.concatenate(groups, axis=0).astype(bf16)
        fs = jnp.dot(xs, wc_ref[c], preferred_element_type=f32)
        pos = jnp.broadcast_to(pos_ref[c], (SUBLANES, CMP_BLOCK * HEAD_DIM)).astype(bf16)
        pos_bias = jnp.dot(pos, pw_ref[c], preferred_element_type=f32)[0:1, :]
        for g in range(N_KV):
            first = fs[g * nc:(g + 1) * nc, :HEAD_DIM]
            second = fs[g * nc:(g + 1) * nc, HEAD_DIM:]
            kc_ref[c * N_KV + g] = first + pltpu.roll(second, nc - 1, 0) + pos_bias


def _kc_call(x_arrays, x_specs, nseq, nc, wc, pw, pos, page_table=None):
    n_pieces = len(x_arrays)
    nsp = 0 if page_table is None else 1
    const = lambda shape: pl.BlockSpec(shape, lambda b, *_: (0,) * len(shape))
    grid_spec = pltpu.PrefetchScalarGridSpec(
        num_scalar_prefetch=nsp,
        grid=(nseq,),
        in_specs=list(x_specs) + [
            const((2, CMP_STRIDE * HEAD_DIM, 2 * HEAD_DIM)),
            const((2, CMP_BLOCK * HEAD_DIM, HEAD_DIM)),
            const((2, 1, CMP_BLOCK * HEAD_DIM)),
        ],
        out_specs=pl.BlockSpec((None, 2 * N_KV, nc, HEAD_DIM), lambda b, *_: (b, 0, 0, 0)),
        scratch_shapes=[pltpu.VMEM((nc * KC_PITCH, HEAD_DIM), f32)],
    )
    args = ([] if page_table is None else [page_table]) + list(x_arrays) + [wc, pw, pos]
    return pl.pallas_call(
        functools.partial(_kc_kernel, n_pieces=n_pieces),
        grid_spec=grid_spec,
        out_shape=jax.ShapeDtypeStruct((nseq, 2 * N_KV, nc, HEAD_DIM), f32),
        compiler_params=_params("parallel"),
        name="kc_sample" if nsp else "kc_prompt",
    )(*args)


def _pos_cols(n_rows, step):
    r = lax.broadcasted_iota(jnp.int32, (n_rows, HEAD_DIM), 0) * step
    lane = lax.broadcasted_iota(jnp.int32, (n_rows, HEAD_DIM), 1)
    low_bits = SLC_BLOCK.bit_length() - 1
    hi = (r >> low_bits) << low_bits
    return jnp.where(lane == 0, hi, jnp.where(lane == 1, r - hi, 0)).astype(f32).astype(bf16)


def _keys_with_pos(k, step=1):
    return jnp.concatenate([k.astype(bf16), _pos_cols(k.shape[0], step)], axis=1)


def _ones_col(n_rows):
    lane = lax.broadcasted_iota(jnp.int32, (n_rows, HEAD_DIM), 1)
    return jnp.where(lane == 0, 1.0, 0.0).astype(bf16)


def _vals_with_ones(v):
    return jnp.concatenate([v.astype(bf16), _ones_col(v.shape[0])], axis=1)


def _exp_rows(s, mask_bias):
    s = s + mask_bias
    m = jnp.max(s, axis=-1, keepdims=True)
    e = jnp.exp(s - m)
    inv = jnp.where(m > 0.5 * NEG, 1.0 / jnp.maximum(jnp.sum(e, axis=-1, keepdims=True), 1e-30), 0.0)
    return e, inv


def _select_blocks(psum, qpos, tb):
    lanes_per_blk = SLC_BLOCK // CMP_STRIDE
    n_lane = psum.shape[1]
    imp = psum + pltpu.roll(psum, 1, 1)
    for back in range(1, lanes_per_blk):
        imp = imp + pltpu.roll(psum, n_lane - back, 1)
    lane = lax.broadcasted_iota(jnp.int32, (tb, n_lane), 1)
    lane_f = lane.astype(f32)
    blk = lane >> (lanes_per_blk.bit_length() - 1)
    cur = qpos >> (SLC_BLOCK.bit_length() - 1)
    cand = ((lane & (lanes_per_blk - 1)) == 0) & (blk >= 1) & (blk < cur)
    score = jnp.where(cand, imp, NEG)
    sel = (lane == 0) & (cur > 0)
    for _ in range(TOPK - 2):
        best = jnp.max(score, axis=-1, keepdims=True)
        first = jnp.min(jnp.where(score == best, lane_f, float(n_lane)), axis=-1, keepdims=True)
        pick = (lane_f == first) & (best > 0.5 * NEG)
        sel = sel | pick
        score = jnp.where(pick, NEG, score)
    return jnp.where(sel, 1.0, 0.0).astype(bf16)


def _attn_core(seqs, qpos, *, tb, slc_rows, win_row0, win_pos0):
    scale = 1.0 / math.sqrt(HEAD_DIM)
    units = [(seq, g) for seq in seqs for g in range(N_KV)]
    n_cmp = seqs[0][2].shape[1]
    slc_shift = SLC_BLOCK.bit_length() - 1
    cur = qpos >> slc_shift
    lane_c = lax.broadcasted_iota(jnp.int32, (tb, n_cmp), 1)
    bias_c = jnp.where(qpos - (CMP_STRIDE * lane_c + (CMP_BLOCK - 1)) >= 0, 0.0, NEG)
    lane_q = lax.broadcasted_iota(jnp.int32, (tb, HEAD_DIM), 1)
    hs = lambda x, p: x[p * tb:(p + 1) * tb]

    qgs = []
    for (q, *_), g in units:
        parts = []
        for p in range(HPG):
            h = g * HPG + p
            slope = 2.0 ** (-8.0 * (h + 1) / N_HEADS)
            parts.append(jnp.concatenate([q[:, h * HEAD_DIM:(h + 1) * HEAD_DIM] * scale,
                                          jnp.where(lane_q < 2, slope, 0.0)], axis=1))
        qgs.append(jnp.concatenate(parts, axis=0).astype(bf16))

    def scores(keys):
        return [_nt(qg, k) for qg, k in zip(qgs, keys)]

    def attend(ss, vals, biases):
        sps = [[hs(s, p) + b for p in range(HPG)] for s, b in zip(ss, biases)]
        ms = [[jnp.max(sp, axis=-1, keepdims=True) for sp in row] for row in sps]
        es = [jnp.concatenate([jnp.exp(sp - m) for sp, m in zip(row, mrow)], axis=0).astype(bf16)
              for row, mrow in zip(sps, ms)]
        pvs = [jnp.dot(e, v, preferred_element_type=f32) for e, v in zip(es, vals)]
        return [pv[:, :HEAD_DIM] / jnp.maximum(pv[:, HEAD_DIM:HEAD_DIM + 1], 1e-30) for pv in pvs]

    ss = [_nt(qg, _keys_with_pos(seq[2][g], CMP_STRIDE)) for qg, (seq, g) in zip(qgs, units)]
    p_c = []
    for s in ss:
        ps = []
        for p in range(HPG):
            e, inv = _exp_rows(hs(s, p), bias_c)
            ps.append(e * inv)
        p_c.append(ps)
    o_cmp = [jnp.dot(jnp.concatenate(ps, axis=0).astype(bf16), seq[2][N_KV + g].astype(bf16),
                     preferred_element_type=f32) for ps, (seq, g) in zip(p_c, units)]
    selm = _select_blocks(jnp.concatenate([ps[0] + ps[1] + ps[2] + ps[3] for ps in p_c], axis=0),
                          jnp.concatenate([qpos] * len(units), axis=0), len(units) * tb)

    ss_slc = scores([seq[3][g, 0:slc_rows, :] for seq, g in units])
    kpos = win_pos0 + lax.broadcasted_iota(jnp.int32, (1, WIN_KEYS), 1)
    d = qpos - kpos
    bias_w = jnp.where((d >= 0) & (d <= WINDOW) & (kpos >= 0), 0.0, NEG)
    o_win = attend(scores([seq[5][g, pl.ds(win_row0, WIN_KEYS), :] for seq, g in units]),
                   [seq[6][g, pl.ds(win_row0, WIN_KEYS), :] for seq, g in units], [bias_w] * len(units))

    kpos = lax.broadcasted_iota(jnp.int32, (1, slc_rows), 1)
    kblk = kpos >> slc_shift
    hot = lax.broadcasted_iota(jnp.int32, (n_cmp, slc_rows), 0) == kblk * (SLC_BLOCK // CMP_STRIDE)
    chosen = jnp.dot(selm, jnp.where(hot, 1.0, 0.0).astype(bf16), preferred_element_type=f32)
    local = (kblk == cur) & (qpos - kpos >= 0)
    o_slc = attend(ss_slc, [seq[4][g, 0:slc_rows, :] for seq, g in units],
                   [jnp.where((chosen[u * tb:(u + 1) * tb] > 0.5) | local, 0.0, NEG) for u in range(len(units))])

    outs = []
    for si, seq in enumerate(seqs):
        sg = seq[1]
        cols = []
        for g in range(N_KV):
            u = si * N_KV + g
            for p in range(HPG):
                h = g * HPG + p
                cols.append(sg[:, h:h + 1] * hs(o_cmp[u], p) + sg[:, N_HEADS + h:N_HEADS + h + 1] * hs(o_slc[u], p)
                            + sg[:, 2 * N_HEADS + h:2 * N_HEADS + h + 1] * hs(o_win[u], p))
        outs.append(jnp.concatenate(cols, axis=1))
    return outs


def _attn_scratch(slc_rows, win_rows):
    return [pltpu.VMEM((N_KV, rows, 2 * HEAD_DIM), bf16) for rows in (slc_rows, slc_rows, win_rows, win_rows)]


def _attn_prompt_kernel(q_ref, kvs_ref, kvw_ref, gn_ref, kc_ref, o_ref, sk, sv, wk, wv, *, T, sb):
    tb = WIN_Q_BLOCK
    i = sb * (SLC_CHUNK // tb) + pl.program_id(1)

    @pl.when(pl.program_id(1) == 0)
    def _():
        for g in range(N_KV):
            for src, kdst, vdst in ((kvs_ref, sk, sv), (kvw_ref, wk, wv)):
                kdst[g] = _keys_with_pos(src[:, g * HEAD_DIM:(g + 1) * HEAD_DIM])
                vdst[g] = _vals_with_ones(src[:, (N_KV + g) * HEAD_DIM:(N_KV + g + 1) * HEAD_DIM])

    qpos = i * tb + lax.broadcasted_iota(jnp.int32, (tb, 1), 0)
    win0 = pl.multiple_of(jnp.clip(i * tb - WINDOW, 0, T - WIN_KEYS), LANES)
    o_ref[...] = _attn_core(
        [(q_ref[...], jax.nn.sigmoid(gn_ref[:, 0:LANES]), kc_ref, sk, sv, wk, wv)], qpos, tb=tb,
        slc_rows=(sb + 1) * SLC_CHUNK, win_row0=win0, win_pos0=win0)[0]


def _attn_prompt(z3, kc, sb):
    B, T, _ = z3.shape
    assert T % SLC_CHUNK == 0 and T >= WIN_KEYS
    tb = WIN_Q_BLOCK
    nq = SLC_CHUNK // tb
    return pl.pallas_call(
        functools.partial(_attn_prompt_kernel, T=T, sb=sb),
        grid=(B, nq),
        in_specs=[
            pl.BlockSpec((None, tb, D_ATT), lambda b, i: (b, sb * nq + i, COL_Q // D_ATT)),
            pl.BlockSpec((None, T, KV_W), lambda b, i: (b, 0, COL_KVS // KV_W)),
            pl.BlockSpec((None, T, KV_W), lambda b, i: (b, 0, COL_KVW // KV_W)),
            pl.BlockSpec((None, tb, GN_PAD), lambda b, i: (b, sb * nq + i, COL_GN // GN_PAD)),
            pl.BlockSpec((None, 2 * N_KV, T // CMP_STRIDE, HEAD_DIM), lambda b, i: (b, 0, 0, 0)),
        ],
        out_specs=pl.BlockSpec((None, tb, D_ATT), lambda b, i: (b, i, 0)),
        out_shape=jax.ShapeDtypeStruct((B, SLC_CHUNK, D_ATT), f32),
        scratch_shapes=_attn_scratch(T, T),
        compiler_params=_params("parallel", "arbitrary"),
        name=f"attn_prompt_{sb}",
    )(z3, z3, z3, z3, kc)


def _attn_sample_kernel(*refs, n_seq, n_pages, page, tq, tb):
    zq_ref, kc_ref, win_ref = refs[1:4]
    page_refs = refs[4:4 + n_seq * n_pages]
    o_ref, skw_ref = refs[4 + n_seq * n_pages:6 + n_seq * n_pages]
    scratch = refs[6 + n_seq * n_pages:]
    past = n_pages * page
    wbuf = win_ref.shape[1] // (2 * N_KV)
    seqs = [_stage_sample_seq(zq_ref.at[s], kc_ref.at[s], win_ref.at[s], page_refs[s * n_pages:(s + 1) * n_pages],
                              skw_ref.at[s], *scratch[4 * s:4 * s + 4], page=page, tq=tq, tb=tb)
            for s in range(n_seq)]
    qpos = past + lax.broadcasted_iota(jnp.int32, (tb, 1), 0)
    outs = _attn_core(seqs, qpos, tb=tb, slc_rows=past + LANES, win_row0=0, win_pos0=past - wbuf)
    for s in range(n_seq):
        o_ref[s] = outs[s]


def _stage_sample_seq(zq_ref, kc_ref, win_ref, page_refs, skw_ref, sk, sv, wk, wv, *, page, tq, tb):
    n_pages = len(page_refs)
    per_tok = 2 * N_KV
    past = n_pages * page
    wbuf = win_ref.shape[0] // per_tok
    q0 = 0
    c_kvs = q0 + D_ATT + KV_W
    c_kvw = c_kvs + KV_W
    c_gn = c_kvw + KV_W

    def new_rows(col):
        return jnp.concatenate([zq_ref[:, col:col + HEAD_DIM], jnp.zeros((LANES - tb, HEAD_DIM), f32)],
                               axis=0).astype(bf16)

    for g in range(N_KV):
        for j in range(n_pages):
            rows = slice(j * page, (j + 1) * page)
            sk[g, rows, 0:HEAD_DIM] = page_refs[j][pl.ds(g, page, stride=per_tok), :].astype(bf16)
            sv[g, rows, 0:HEAD_DIM] = page_refs[j][pl.ds(N_KV + g, page, stride=per_tok), :].astype(bf16)
        sk[g, past:past + LANES, 0:HEAD_DIM] = new_rows(c_kvs + g * HEAD_DIM)
        sv[g, past:past + LANES, 0:HEAD_DIM] = new_rows(c_kvs + (N_KV + g) * HEAD_DIM)
        sk[g, :, HEAD_DIM:] = _pos_cols(past + LANES, 1)
        sv[g, :, HEAD_DIM:] = _ones_col(past + LANES)
        wk[g, 0:wbuf, 0:HEAD_DIM] = win_ref[pl.ds(g, wbuf, stride=per_tok), :].astype(bf16)
        wv[g, 0:wbuf, 0:HEAD_DIM] = win_ref[pl.ds(N_KV + g, wbuf, stride=per_tok), :].astype(bf16)
        wk[g, wbuf:wbuf + LANES, 0:HEAD_DIM] = new_rows(c_kvw + g * HEAD_DIM)
        wv[g, wbuf:wbuf + LANES, 0:HEAD_DIM] = new_rows(c_kvw + (N_KV + g) * HEAD_DIM)
        wk[g, :, HEAD_DIM:] = _pos_cols(wbuf + LANES, 1)
        wv[g, :, HEAD_DIM:] = _ones_col(wbuf + LANES)
    skw_ref[0:(wbuf - tq) * per_tok, :] = win_ref[tq * per_tok:wbuf * per_tok, :]
    for t in range(tq):
        for cg in range(per_tok):
            r = (wbuf - tq + t) * per_tok + cg
            skw_ref[r:r + 1, :] = zq_ref[t:t + 1, c_kvw + cg * HEAD_DIM:c_kvw + (cg + 1) * HEAD_DIM]

    return zq_ref[:, q0:q0 + D_ATT], jax.nn.sigmoid(zq_ref[:, c_gn:c_gn + LANES]), kc_ref, sk, sv, wk, wv


def _attn_sample(zq, kc, cache_win, cache_slc, page_table, *, tq):
    DB, tb, zw = zq.shape
    per_tok = 2 * N_KV
    n_pages = page_table.shape[1]
    page = cache_slc.shape[1] // per_tok
    past = n_pages * page
    wbuf = cache_win.shape[1] // per_tok
    n_seq = 2 if DB % 2 == 0 else 1
    assert wbuf + LANES == WIN_KEYS and tq <= tb and tq < CMP_STRIDE and (tq * per_tok) % SUBLANES == 0
    page_specs = [pl.BlockSpec((None, page * per_tok, HEAD_DIM),
                               functools.partial(lambda b, pt, s, j: (pt[b * n_seq + s, j], 0, 0), s=s, j=j))
                  for s in range(n_seq) for j in range(n_pages)]
    grid_spec = pltpu.PrefetchScalarGridSpec(
        num_scalar_prefetch=1,
        grid=(DB // n_seq,),
        in_specs=[
            pl.BlockSpec((n_seq, tb, zw), lambda b, pt: (b, 0, 0)),
            pl.BlockSpec((n_seq, 2 * N_KV, past // CMP_STRIDE, HEAD_DIM), lambda b, pt: (b, 0, 0, 0)),
            pl.BlockSpec((n_seq, wbuf * per_tok, HEAD_DIM), lambda b, pt: (b, 0, 0)),
        ] + page_specs,
        out_specs=[
            pl.BlockSpec((n_seq, tb, D_ATT), lambda b, pt: (b, 0, 0)),
            pl.BlockSpec((n_seq, wbuf * per_tok, HEAD_DIM), lambda b, pt: (b, 0, 0)),
        ],
        scratch_shapes=_attn_scratch(past + LANES, wbuf + LANES) * n_seq,
    )
    return pl.pallas_call(
        functools.partial(_attn_sample_kernel, n_seq=n_seq, n_pages=n_pages, page=page, tq=tq, tb=tb),
        grid_spec=grid_spec,
        out_shape=[jax.ShapeDtypeStruct((DB, tb, D_ATT), f32),
                   jax.ShapeDtypeStruct((DB, wbuf * per_tok, HEAD_DIM), f32)],
        compiler_params=_params("parallel"),
        name="attn_sample",
    )(page_table, zq, kc, cache_win, *([cache_slc] * (n_seq * n_pages)))


def _merge_kernel(x_ref, yr_ref, o_ref, gm1_ref, gm2_ref, wr_ref, wa_ref, wo_ref, out_ref):
    m = (jax.nn.sigmoid(gm1_ref[...]) * jnp.dot(yr_ref[...].astype(bf16), wr_ref[...], preferred_element_type=f32)
         + jax.nn.sigmoid(gm2_ref[...]) * jnp.dot(o_ref[...].astype(bf16), wa_ref[...], preferred_element_type=f32))
    out_ref[...] = x_ref[...] + jnp.dot(m.astype(bf16), wo_ref[...], preferred_element_type=f32)


def _merge(x2d, yr, o, z2d, wr, wa, wo):
    rows = x2d.shape[0]
    tm = min(256, rows)
    const = lambda shape: pl.BlockSpec(shape, lambda i: (0, 0), pipeline_mode=pl.Buffered(1))
    return pl.pallas_call(
        _merge_kernel,
        grid=(rows // tm,),
        in_specs=[
            pl.BlockSpec((tm, D_MODEL), lambda i: (i, 0)),
            pl.BlockSpec((tm, D_RNN), lambda i: (i, 0)),
            pl.BlockSpec((tm, D_ATT), lambda i: (i, 0)),
            pl.BlockSpec((tm, D_MODEL), lambda i: (i, COL_GM // D_MODEL)),
            pl.BlockSpec((tm, D_MODEL), lambda i: (i, COL_GM // D_MODEL + 1)),
            const((D_RNN, D_MODEL)), const((D_ATT, D_MODEL)), const((D_MODEL, D_MODEL)),
        ],
        out_specs=pl.BlockSpec((tm, D_MODEL), lambda i: (i, 0)),
        out_shape=jax.ShapeDtypeStruct((rows, D_MODEL), f32),
        compiler_params=_params("parallel"),
        name="merge",
    )(x2d, yr, o, z2d, z2d, wr, wa, wo)


def _ffn_kernel(x_ref, prev_ref, gf_ref, wg_ref, wu_ref, wd_ref, cw_ref, cb_ref, gl_ref,
                y_ref, fc_ref, h_ref, acc_ref, gcat, carry, *, S, rows, final_norm):
    ti, fi = pl.program_id(1), pl.program_id(2)
    k1 = FFN_CONV - 1
    pad = _round_up(k1 * S, SUBLANES)

    @pl.when(fi == 0)
    def _():
        h_ref[...] = _rms(x_ref[...], gf_ref[...]).astype(bf16)
        acc_ref[...] = jnp.zeros_like(acc_ref)

    @pl.when(ti == 0)
    def _():
        carry[fi, pad - k1 * S:pad, :] = prev_ref[...]

    h = h_ref[...]
    gate = jnp.dot(h, wg_ref[...], preferred_element_type=f32)
    gcat[pad - k1 * S:pad, :] = carry[fi, pad - k1 * S:pad, :]
    gcat[pad:pad + rows, :] = gate
    u = cb_ref[...] + cw_ref[k1:k1 + 1, :] * gate
    for j in range(k1):
        off = pad - (k1 - j) * S
        u = u + cw_ref[j:j + 1, :] * gcat[off:off + rows, :]
    tail = gcat[pad + rows - k1 * S:pad + rows, :]
    carry[fi, pad - k1 * S:pad, :] = tail
    fc_ref[...] = tail
    up = jnp.dot(h, wu_ref[...], preferred_element_type=f32)
    acc_ref[...] += jnp.dot((_gelu(u) * up).astype(bf16), wd_ref[...], preferred_element_type=f32)

    @pl.when(fi == pl.num_programs(2) - 1)
    def _():
        out = x_ref[...] + acc_ref[...]
        y_ref[...] = _rms(out, gl_ref[...]) if final_norm else out


def _ffn(x3, prev, g_ffn, wg, wu, wd, cw, cb, g_fin, *, S, Tt, final_norm):
    nb, rper, _ = x3.shape
    rows = Tt * S
    tf = 768
    nf = D_FF // tf
    k1 = FFN_CONV - 1
    pad = _round_up(k1 * S, SUBLANES)
    assert rper % rows == 0 and Tt >= k1
    return pl.pallas_call(
        functools.partial(_ffn_kernel, S=S, rows=rows, final_norm=final_norm),
        grid=(nb, rper // rows, nf),
        in_specs=[
            pl.BlockSpec((None, rows, D_MODEL), lambda b, t, f: (b, t, 0)),
            pl.BlockSpec((None, k1 * S, tf), lambda b, t, f: (b, 0, f)),
            pl.BlockSpec((1, D_MODEL), lambda b, t, f: (0, 0)),
            pl.BlockSpec((D_MODEL, tf), lambda b, t, f: (0, f)),
            pl.BlockSpec((D_MODEL, tf), lambda b, t, f: (0, f)),
            pl.BlockSpec((tf, D_MODEL), lambda b, t, f: (f, 0)),
            pl.BlockSpec((FFN_CONV, tf), lambda b, t, f: (0, f)),
            pl.BlockSpec((1, tf), lambda b, t, f: (0, f)),
            pl.BlockSpec((1, D_MODEL), lambda b, t, f: (0, 0)),
        ],
        out_specs=[
            pl.BlockSpec((None, rows, D_MODEL), lambda b, t, f: (b, t, 0)),
            pl.BlockSpec((None, None, k1 * S, tf), lambda b, t, f: (b, t, 0, f)),
        ],
        out_shape=[
            jax.ShapeDtypeStruct((nb, rper, D_MODEL), f32),
            jax.ShapeDtypeStruct((nb, rper // rows, k1 * S, D_FF), f32),
        ],
        scratch_shapes=[
            pltpu.VMEM((rows, D_MODEL), bf16),
            pltpu.VMEM((rows, D_MODEL), f32),
            pltpu.VMEM((pad + rows, tf), f32),
            pltpu.VMEM((nf, pad, tf), f32),
        ],
        compiler_params=_params("parallel", "arbitrary", "arbitrary"),
        name="ffn",
    )(x3, prev, g_ffn, wg, wu, wd, cw, cb, g_fin)


def _time_major(a):
    return jnp.swapaxes(a, 0, 1).reshape(1, a.shape[0] * a.shape[1], a.shape[2])


def _batch_major(a, db):
    return jnp.swapaxes(a.reshape(a.shape[1] // db, db, a.shape[2]), 0, 1)


def kernel(x_prompt, x_sample, cache_kv_cmp, cache_kv_slc, cache_kv_win, state_rnn_h, state_rnn_conv, state_ffn_conv, page_table, norm_mix, w_in, rnn_conv_w, rnn_conv_b, rnn_wa, rnn_ba, rnn_wx, rnn_bx, rnn_lambda, cmp_w, cmp_pos, w_proj_rnn, w_proj_att, w_out, norm_ffn, ffn_w_gate, ffn_w_up, ffn_conv_w, ffn_conv_b, ffn_w_down, norm_final):
    B, T, _ = x_prompt.shape
    DB, Tq, _ = x_sample.shape
    depth = w_in.shape[0]
    n_pool, page = cache_kv_cmp.shape[1:3]
    n_pages = page_table.shape[1]
    past = n_pages * page
    wbuf = cache_kv_win.shape[2]
    tb = SUBLANES
    assert page % CMP_STRIDE == 0 and page % SLC_BLOCK == 0 and Tq >= RNN_CONV - 1 and Tq <= tb
    assert Tq & (Tq - 1) == 0 and DB % SUBLANES == 0 and T % 256 == 0
    kv_shape = (2, N_KV, HEAD_DIM)

    xp = x_prompt.reshape(B * T, D_MODEL)
    xs = _time_major(x_sample)[0]
    pst = [[] for _ in range(6)]
    sst = [[] for _ in range(6)]
    g_out = norm_final.reshape(1, D_MODEL)
    for l in range(depth):
        w = w_in[l]
        widths = (D_RNN, D_RNN, D_ATT, KV_W, KV_W, KV_W, 3 * N_HEADS, 2 * D_MODEL)
        cuts = [0]
        for wd_ in widths:
            cuts.append(cuts[-1] + wd_)
        part = lambda k: w[:, cuts[k]:cuts[k + 1]]
        w_cat = jnp.concatenate(
            [part(0), part(1), part(7), part(2), part(3), part(4), part(5),
             jnp.pad(part(6), ((0, 0), (0, GN_PAD - 3 * N_HEADS)))], axis=1).astype(bf16)
        g_mix = norm_mix[l].reshape(1, D_MODEL)
        cw, cb = rnn_conv_w[l], rnn_conv_b[l].reshape(1, D_RNN)
        wgate = jnp.concatenate([rnn_wa[l], rnn_wx[l]], axis=-1).astype(bf16)
        ba, bx, lam = (v[l].reshape(1, D_RNN) for v in (rnn_ba, rnn_bx, rnn_lambda))
        cmpw = cmp_w[l].astype(bf16)
        wc = jnp.concatenate([cmpw[:, :CMP_STRIDE].reshape(2, CMP_STRIDE * HEAD_DIM, HEAD_DIM),
                              cmpw[:, CMP_STRIDE:].reshape(2, CMP_STRIDE * HEAD_DIM, HEAD_DIM)], axis=-1)
        pw = cmpw.reshape(2, CMP_BLOCK * HEAD_DIM, HEAD_DIM)
        pos = cmp_pos[l].reshape(2, 1, CMP_BLOCK * HEAD_DIM)
        wr, wa, wo = w_proj_rnn[l].astype(bf16), w_proj_att[l].astype(bf16), w_out[l].astype(bf16)
        g_ffn = norm_ffn[l].reshape(1, D_MODEL)
        fwg, fwu, fwd = ffn_w_gate[l].astype(bf16), ffn_w_up[l].astype(bf16), ffn_w_down[l].astype(bf16)
        fcw, fcb = ffn_conv_w[l], ffn_conv_b[l].reshape(1, D_FF)
        last = l == depth - 1

        per_tok = 2 * N_KV
        z, kvc_rows, kvs_rows, kvw_rows = _inproj(xp, g_mix, w_cat)
        z3 = z.reshape(B, T, D_Z)
        y_rnn, h_last, rconv = _rglru(z3, jnp.zeros((B, RNN_CONV - 1, D_RNN), f32), jnp.zeros((B, 1, D_RNN), f32),
                                      cw, cb, wgate, ba, bx, lam, S=1, Tt=256)
        kc = _kc_call([kvc_rows.reshape(B, T * per_tok, HEAD_DIM)],
                      [pl.BlockSpec((None, T * per_tok, HEAD_DIM), lambda b: (b, 0, 0))],
                      B, T // CMP_STRIDE, wc, pw, pos)
        o = jnp.concatenate([_attn_prompt(z3, kc, sb) for sb in range(T // SLC_CHUNK)], axis=1)
        x1 = _merge(xp, y_rnn.reshape(B * T, D_RNN), o.reshape(B * T, D_ATT), z, wr, wa, wo)
        y, fconv = _ffn(x1.reshape(B, T, D_MODEL), jnp.zeros((B, FFN_CONV - 1, D_FF), f32), g_ffn, fwg, fwu, fwd,
                        fcw, fcb, g_out, S=1, Tt=512, final_norm=last)
        xp = y.reshape(B * T, D_MODEL)
        wlen = min(WINDOW, T)
        for lst, v in zip(pst, (kvc_rows.reshape(B, T, *kv_shape), kvs_rows.reshape(B, T, *kv_shape),
                                kvw_rows.reshape(B, T, *kv_shape)[:, T - wlen:], h_last[:, 0], rconv, fconv[:, -1])):
            lst.append(v)

        zs = _inproj(xs, g_mix, w_cat)[0]
        zs3 = zs.reshape(1, Tq * DB, D_Z)
        ys_rnn, hs_last, sconv = _rglru(zs3, _time_major(state_rnn_conv[l]), state_rnn_h[l][None],
                                        cw, cb, wgate, ba, bx, lam, S=DB, Tt=Tq)
        zq = _batch_major(zs3[:, :, COL_Q:], DB)
        zq = jnp.pad(zq, ((0, 0), (0, tb - Tq), (0, 0)))
        kcs = _kc_call(
            [cache_kv_cmp[l].reshape(n_pool, page * per_tok, HEAD_DIM)] * n_pages,
            [pl.BlockSpec((None, page * per_tok, HEAD_DIM),
                          functools.partial(lambda b, pt, j: (pt[b, j], 0, 0), j=j)) for j in range(n_pages)],
            DB, past // CMP_STRIDE, wc, pw, pos, page_table=page_table)
        os_, s_win = _attn_sample(zq, kcs, cache_kv_win[l].reshape(DB, wbuf * per_tok, HEAD_DIM),
                                  cache_kv_slc[l].reshape(n_pool, page * per_tok, HEAD_DIM), page_table, tq=Tq)
        os_tm = _time_major(os_[:, :Tq])[0]
        xs1 = _merge(xs, ys_rnn[0], os_tm, zs, wr, wa, wo)
        ys, sfconv = _ffn(xs1[None], _time_major(state_ffn_conv[l]), g_ffn, fwg, fwu, fwd, fcw, fcb, g_out,
                          S=DB, Tt=Tq, final_norm=last)
        xs = ys[0]
        c0 = D_ATT
        for lst, v in zip(sst, (zq[:, :Tq, c0:c0 + KV_W].reshape(DB, Tq, *kv_shape),
                                zq[:, :Tq, c0 + KV_W:c0 + 2 * KV_W].reshape(DB, Tq, *kv_shape),
                                s_win.reshape(DB, wbuf, *kv_shape), hs_last[0],
                                _batch_major(sconv, DB), _batch_major(sfconv[:, -1], DB))):
            lst.append(v)

    y_prompt = xp.reshape(B, T, D_MODEL)
    y_sample = _batch_major(xs[None], DB)
    return (y_prompt, y_sample, *[jnp.stack(v) for v in pst], *[jnp.stack(v) for v in sst])
```

```python
import functools
import math

import jax
import jax.numpy as jnp
from jax import lax
from jax.experimental import pallas as pl
from jax.experimental.pallas import tpu as pltpu

D_MODEL = 2048
D_RNN = D_MODEL // 2
RNN_BLOCKS = 8
RNN_BLK = D_RNN // RNN_BLOCKS
RNN_CONV = 4
LRU_C = 8.0
HEAD_DIM = 128
D_ATT = D_MODEL // 2
N_HEADS = D_ATT // HEAD_DIM
N_KV = 2
HPG = N_HEADS // N_KV
KV_W = 2 * N_KV * HEAD_DIM
CMP_BLOCK = 32
CMP_STRIDE = 16
SLC_BLOCK = 64
TOPK = 8
WINDOW = 512
WIN_Q_BLOCK = 128
D_FF = 3 * D_MODEL
FFN_CONV = 3
RMS_EPS = 1e-6
NEG = -1e30

COL_XR = 0
COL_GR = COL_XR + D_RNN
COL_GM = COL_GR + D_RNN
COL_Q = COL_GM + 2 * D_MODEL
COL_KVC = COL_Q + D_ATT
COL_KVS = COL_KVC + KV_W
COL_KVW = COL_KVS + KV_W
COL_GN = COL_KVW + KV_W
GN_PAD = KV_W
D_Z = COL_GN + GN_PAD

VMEM_LIMIT_BYTES = 56 * 1024 * 1024
SUBLANES = 8
LANES = 128

WIN_KEYS = WINDOW + WIN_Q_BLOCK
KC_PITCH = CMP_STRIDE * 2 * N_KV + SUBLANES
SLC_CHUNK = 512

f32 = jnp.float32
bf16 = jnp.bfloat16


def _round_up(x, m):
    return (x + m - 1) // m * m


def _params(*sem):
    return pltpu.CompilerParams(dimension_semantics=sem, vmem_limit_bytes=VMEM_LIMIT_BYTES)


def _rms(x, g):
    return (x * lax.rsqrt(jnp.mean(x * x, axis=-1, keepdims=True) + RMS_EPS)) * g


def _gelu(x):
    return x * (0.5 * (1.0 + jnp.tanh(math.sqrt(2.0 / math.pi) * (x + 0.044715 * (x * x * x)))))


def _nt(a, b):
    return lax.dot_general(a, b, (((1,), (1,)), ((), ())), preferred_element_type=f32)


def _inproj_kernel(x_ref, g_ref, w_ref, o_ref, *refs, tm, tn):
    kv_refs, h_ref = refs[:3], refs[3]

    @pl.when(pl.program_id(1) == 0)
    def _():
        h_ref[...] = _rms(x_ref[...], g_ref[...]).astype(bf16)

    o_ref[...] = jnp.dot(h_ref[...], w_ref[...], preferred_element_type=f32)

    per_tok = 2 * N_KV
    for col, kv_ref in zip((COL_KVC, COL_KVS, COL_KVW), kv_refs):
        lo = col % tn

        @pl.when(pl.program_id(1) == col // tn)
        def _(lo=lo, kv_ref=kv_ref):
            for cg in range(per_tok):
                kv_ref[pl.ds(cg, tm, stride=per_tok), :] = o_ref[:, lo + cg * HEAD_DIM:lo + (cg + 1) * HEAD_DIM]


def _inproj(x2d, g, w_cat):
    rows = x2d.shape[0]
    tm = min(1024, rows)
    tn = 1024
    per_tok = 2 * N_KV
    assert rows % tm == 0 and D_Z % tn == 0
    assert all(c % tn + KV_W <= tn for c in (COL_KVC, COL_KVS, COL_KVW))
    kv_spec = pl.BlockSpec((tm * per_tok, HEAD_DIM), lambda i, j: (i, 0))
    kv_shape = jax.ShapeDtypeStruct((rows * per_tok, HEAD_DIM), f32)
    return pl.pallas_call(
        functools.partial(_inproj_kernel, tm=tm, tn=tn),
        grid=(rows // tm, D_Z // tn),
        in_specs=[
            pl.BlockSpec((tm, D_MODEL), lambda i, j: (i, 0)),
            pl.BlockSpec((1, D_MODEL), lambda i, j: (0, 0)),
            pl.BlockSpec((D_MODEL, tn), lambda i, j: (0, j)),
        ],
        out_specs=[pl.BlockSpec((tm, tn), lambda i, j: (i, j)), kv_spec, kv_spec, kv_spec],
        out_shape=[jax.ShapeDtypeStruct((rows, D_Z), f32), kv_shape, kv_shape, kv_shape],
        scratch_shapes=[pltpu.VMEM((tm, D_MODEL), bf16)],
        compiler_params=_params("parallel", "arbitrary"),
        name="inproj",
    )(x2d, g, w_cat)


def _rglru_kernel(xr_ref, gr_ref, prev_ref, h0_ref, cw_ref, cb_ref, wg_ref, ba_ref, bx_ref, lam_ref,
                  y_ref, hl_ref, nc_ref, xcat, hc, sa, su, *, S, Tt):
    rows = Tt * S
    k1 = RNN_CONV - 1
    pad = _round_up(k1 * S, SUBLANES)
    ipad = max((Tt // 2) * S, SUBLANES)

    @pl.when(pl.program_id(1) == 0)
    def _():
        xcat[pad - k1 * S:pad, :] = prev_ref[...]
        hc[...] = h0_ref[...]
        for k in range(2):
            sa[k, 0:ipad, :] = jnp.ones((ipad, D_RNN), f32)
            su[k, 0:ipad, :] = jnp.zeros((ipad, D_RNN), f32)

    xcat[pad:pad + rows, :] = xr_ref[...]
    xc = cb_ref[...] + cw_ref[k1:k1 + 1, :] * xr_ref[...]
    for j in range(k1):
        off = pad - (k1 - j) * S
        xc = xc + cw_ref[j:j + 1, :] * xcat[off:off + rows, :]
    tail = xcat[pad + rows - k1 * S:pad + rows, :]
    xcat[pad - k1 * S:pad, :] = tail
    nc_ref[...] = tail

    lam = lam_ref[...]
    softplus_neg_lam = jnp.maximum(-lam, 0.0) + jnp.log1p(jnp.exp(-jnp.abs(lam)))
    xcb = xc.astype(bf16)
    for n in range(RNN_BLOCKS):
        sl = slice(n * RNN_BLK, (n + 1) * RNN_BLK)
        gates = jnp.dot(xcb[:, sl], wg_ref[n], preferred_element_type=f32)
        r = jax.nn.sigmoid(gates[:, :RNN_BLK] + ba_ref[:, sl])
        i = jax.nn.sigmoid(gates[:, RNN_BLK:] + bx_ref[:, sl])
        log_a = (-LRU_C) * r * softplus_neg_lam[:, sl]
        a = jnp.exp(log_a)
        one_minus_a2 = -jnp.tanh(log_a) * (a * a + 1.0)
        sa[0, ipad:ipad + rows, sl] = a
        su[0, ipad:ipad + rows, sl] = jnp.sqrt(one_minus_a2) * (i * xc[:, sl])

    n_rounds = Tt.bit_length() - 1
    for rnd in range(n_rounds):
        d = (1 << rnd) * S
        src, dst = rnd % 2, 1 - rnd % 2
        a = sa[src, ipad:ipad + rows, :]
        su[dst, ipad:ipad + rows, :] = a * su[src, ipad - d:ipad - d + rows, :] + su[src, ipad:ipad + rows, :]
        sa[dst, ipad:ipad + rows, :] = a * sa[src, ipad - d:ipad - d + rows, :]
    fin = n_rounds % 2
    a_cum = sa[fin, ipad:ipad + rows, :]
    u_cum = su[fin, ipad:ipad + rows, :]
    if S == 1:
        h = a_cum * hc[...] + u_cum
    else:
        h = (a_cum.reshape(Tt, S, D_RNN) * hc[...][None] + u_cum.reshape(Tt, S, D_RNN)).reshape(rows, D_RNN)
    y_ref[...] = h * _gelu(gr_ref[...])
    su[fin, ipad:ipad + rows, :] = h
    last = su[fin, ipad + rows - S:ipad + rows, :]
    hc[...] = last
    hl_ref[...] = last


def _rglru(z3, prev, h0, cw, cb, wg, ba, bx, lam, *, S, Tt):
    nb, rper, _ = z3.shape
    rows = Tt * S
    assert Tt & (Tt - 1) == 0 and rper % rows == 0 and Tt >= RNN_CONV - 1
    k1 = RNN_CONV - 1
    pad = _round_up(k1 * S, SUBLANES)
    ipad = max((Tt // 2) * S, SUBLANES)
    vec = lambda: pl.BlockSpec((1, D_RNN), lambda b, t: (0, 0))
    return pl.pallas_call(
        functools.partial(_rglru_kernel, S=S, Tt=Tt),
        grid=(nb, rper // rows),
        in_specs=[
            pl.BlockSpec((None, rows, D_RNN), lambda b, t: (b, t, COL_XR // D_RNN)),
            pl.BlockSpec((None, rows, D_RNN), lambda b, t: (b, t, COL_GR // D_RNN)),
            pl.BlockSpec((None, k1 * S, D_RNN), lambda b, t: (b, 0, 0)),
            pl.BlockSpec((None, S, D_RNN), lambda b, t: (b, 0, 0)),
            pl.BlockSpec((RNN_CONV, D_RNN), lambda b, t: (0, 0)),
            vec(),
            pl.BlockSpec((RNN_BLOCKS, RNN_BLK, 2 * RNN_BLK), lambda b, t: (0, 0, 0)),
            vec(), vec(), vec(),
        ],
        out_specs=[
            pl.BlockSpec((None, rows, D_RNN), lambda b, t: (b, t, 0)),
            pl.BlockSpec((None, S, D_RNN), lambda b, t: (b, 0, 0)),
            pl.BlockSpec((None, k1 * S, D_RNN), lambda b, t: (b, 0, 0)),
        ],
        out_shape=[
            jax.ShapeDtypeStruct((nb, rper, D_RNN), f32),
            jax.ShapeDtypeStruct((nb, S, D_RNN), f32),
            jax.ShapeDtypeStruct((nb, k1 * S, D_RNN), f32),
        ],
        scratch_shapes=[
            pltpu.VMEM((pad + rows, D_RNN), f32),
            pltpu.VMEM((S, D_RNN), f32),
            pltpu.VMEM((2, ipad + rows, D_RNN), f32),
            pltpu.VMEM((2, ipad + rows, D_RNN), f32),
        ],
        compiler_params=_params("parallel", "arbitrary"),
        name="rglru",
    )(z3, z3, prev, h0, cw, cb, wg, ba, bx, lam)


def _kc_kernel(*refs, n_seq, n_pieces):
    refs = refs[len(refs) - (n_seq * n_pieces + 4 + n_seq):]
    x_refs = refs[:n_seq * n_pieces]
    wc_ref, pw_ref, pos_ref, kc_ref = refs[n_seq * n_pieces:n_seq * n_pieces + 4]
    xps = refs[n_seq * n_pieces + 4:]
    per_tok = 2 * N_KV
    ch = CMP_STRIDE * per_tok
    for s in range(n_seq):
        nc = 0
        for r in x_refs[s * n_pieces:(s + 1) * n_pieces]:
            for k in range(r.shape[0] // ch):
                xps[s][nc * KC_PITCH:nc * KC_PITCH + ch, :] = r[k * ch:(k + 1) * ch, :]
                nc += 1
    for c in range(2):
        groups = [jnp.concatenate([xps[s][pl.ds(l * per_tok + c * N_KV + g, nc, stride=KC_PITCH), :]
                                   for l in range(CMP_STRIDE)], axis=1) for s in range(n_seq) for g in range(N_KV)]
        xs = jnp.concatenate(groups, axis=0).astype(bf16)
        fs = jnp.dot(xs, wc_ref[c], preferred_element_type=f32)
        pos = jnp.broadcast_to(pos_ref[c], (SUBLANES, CMP_BLOCK * HEAD_DIM)).astype(bf16)
        pos_bias = jnp.dot(pos, pw_ref[c], preferred_element_type=f32)[0:1, :]
        for s in range(n_seq):
            for g in range(N_KV):
                u = s * N_KV + g
                first = fs[u * nc:(u + 1) * nc, :HEAD_DIM]
                second = fs[u * nc:(u + 1) * nc, HEAD_DIM:]
                kc_ref[s, c * N_KV + g] = first + pltpu.roll(second, nc - 1, 0) + pos_bias


def _kc_call(x, n_pieces, nseq, wc, pw, pos, page_table=None):
    assert page_table is not None or n_pieces == 1
    piece_rows = x.shape[1]
    nc = n_pieces * piece_rows // (CMP_STRIDE * 2 * N_KV)
    n_seq = 2 if nseq % 2 == 0 else 1
    nsp = 0 if page_table is None else 1
    const = lambda shape: pl.BlockSpec(shape, lambda b, *_: (0,) * len(shape))

    def piece_index(b, *pt, s, j):
        return (pt[0][b * n_seq + s, j] if pt else b * n_seq + s, 0, 0)

    grid_spec = pltpu.PrefetchScalarGridSpec(
        num_scalar_prefetch=nsp,
        grid=(nseq // n_seq,),
        in_specs=[pl.BlockSpec((None, piece_rows, HEAD_DIM), functools.partial(piece_index, s=s, j=j))
                  for s in range(n_seq) for j in range(n_pieces)] + [
            const((2, CMP_STRIDE * HEAD_DIM, 2 * HEAD_DIM)),
            const((2, CMP_BLOCK * HEAD_DIM, HEAD_DIM)),
            const((2, 1, CMP_BLOCK * HEAD_DIM)),
        ],
        out_specs=pl.BlockSpec((n_seq, 2 * N_KV, nc, HEAD_DIM), lambda b, *_: (b, 0, 0, 0)),
        scratch_shapes=[pltpu.VMEM((nc * KC_PITCH, HEAD_DIM), f32)] * n_seq,
    )
    args = ([] if page_table is None else [page_table]) + [x] * (n_seq * n_pieces) + [wc, pw, pos]
    return pl.pallas_call(
        functools.partial(_kc_kernel, n_seq=n_seq, n_pieces=n_pieces),
        grid_spec=grid_spec,
        out_shape=jax.ShapeDtypeStruct((nseq, 2 * N_KV, nc, HEAD_DIM), f32),
        compiler_params=_params("parallel"),
        name="kc_sample" if nsp else "kc_prompt",
    )(*args)


def _pos_cols(n_rows, step):
    r = lax.broadcasted_iota(jnp.int32, (n_rows, HEAD_DIM), 0) * step
    lane = lax.broadcasted_iota(jnp.int32, (n_rows, HEAD_DIM), 1)
    low_bits = SLC_BLOCK.bit_length() - 1
    hi = (r >> low_bits) << low_bits
    return jnp.where(lane == 0, hi, jnp.where(lane == 1, r - hi, 0)).astype(f32).astype(bf16)


def _keys_with_pos(k, step=1):
    return jnp.concatenate([k.astype(bf16), _pos_cols(k.shape[0], step)], axis=1)


def _ones_col(n_rows):
    lane = lax.broadcasted_iota(jnp.int32, (n_rows, HEAD_DIM), 1)
    return jnp.where(lane == 0, 1.0, 0.0).astype(bf16)


def _vals_with_ones(v):
    return jnp.concatenate([v.astype(bf16), _ones_col(v.shape[0])], axis=1)


def _exp_rows(s, mask_bias):
    s = s + mask_bias
    m = jnp.max(s, axis=-1, keepdims=True)
    e = jnp.exp(s - m)
    inv = jnp.where(m > 0.5 * NEG, 1.0 / jnp.maximum(jnp.sum(e, axis=-1, keepdims=True), 1e-30), 0.0)
    return e, inv


def _select_blocks(psum, qpos, tb):
    lanes_per_blk = SLC_BLOCK // CMP_STRIDE
    n_lane = psum.shape[1]
    imp = psum + pltpu.roll(psum, 1, 1)
    for back in range(1, lanes_per_blk):
        imp = imp + pltpu.roll(psum, n_lane - back, 1)
    lane = lax.broadcasted_iota(jnp.int32, (tb, n_lane), 1)
    lane_f = lane.astype(f32)
    blk = lane >> (lanes_per_blk.bit_length() - 1)
    cur = qpos >> (SLC_BLOCK.bit_length() - 1)
    cand = ((lane & (lanes_per_blk - 1)) == 0) & (blk >= 1) & (blk < cur)
    score = jnp.where(cand, imp, NEG)
    sel = (lane == 0) & (cur > 0)
    for _ in range(TOPK - 2):
        best = jnp.max(score, axis=-1, keepdims=True)
        first = jnp.min(jnp.where(score == best, lane_f, float(n_lane)), axis=-1, keepdims=True)
        pick = (lane_f == first) & (best > 0.5 * NEG)
        sel = sel | pick
        score = jnp.where(pick, NEG, score)
    return jnp.where(sel, 1.0, 0.0).astype(bf16)


def _attn_core(seqs, qpos, *, tb, slc_rows, win_row0, win_pos0):
    scale = 1.0 / math.sqrt(HEAD_DIM)
    units = [(seq, g) for seq in seqs for g in range(N_KV)]
    n_cmp = seqs[0][2].shape[1]
    slc_shift = SLC_BLOCK.bit_length() - 1
    cur = qpos >> slc_shift
    lane_c = lax.broadcasted_iota(jnp.int32, (tb, n_cmp), 1)
    bias_c = jnp.where(qpos - (CMP_STRIDE * lane_c + (CMP_BLOCK - 1)) >= 0, 0.0, NEG)
    lane_q = lax.broadcasted_iota(jnp.int32, (tb, HEAD_DIM), 1)
    hs = lambda x, p: x[p * tb:(p + 1) * tb]

    qgs = []
    for (q, *_), g in units:
        parts = []
        for p in range(HPG):
            h = g * HPG + p
            slope = 2.0 ** (-8.0 * (h + 1) / N_HEADS)
            parts.append(jnp.concatenate([q[:, h * HEAD_DIM:(h + 1) * HEAD_DIM] * scale,
                                          jnp.where(lane_q < 2, slope, 0.0)], axis=1))
        qgs.append(jnp.concatenate(parts, axis=0).astype(bf16))

    def scores(keys):
        return [_nt(qg, k) for qg, k in zip(qgs, keys)]

    def attend(ss, vals, biases):
        sps = [[hs(s, p) + b for p in range(HPG)] for s, b in zip(ss, biases)]
        ms = [[jnp.max(sp, axis=-1, keepdims=True) for sp in row] for row in sps]
        es = [jnp.concatenate([jnp.exp(sp - m) for sp, m in zip(row, mrow)], axis=0).astype(bf16)
              for row, mrow in zip(sps, ms)]
        pvs = [jnp.dot(e, v, preferred_element_type=f32) for e, v in zip(es, vals)]
        return [pv[:, :HEAD_DIM] / jnp.maximum(pv[:, HEAD_DIM:HEAD_DIM + 1], 1e-30) for pv in pvs]

    ss = [_nt(qg, _keys_with_pos(seq[2][g], CMP_STRIDE)) for qg, (seq, g) in zip(qgs, units)]
    p_c = []
    for s in ss:
        ps = []
        for p in range(HPG):
            e, inv = _exp_rows(hs(s, p), bias_c)
            ps.append(e * inv)
        p_c.append(ps)
    o_cmp = [jnp.dot(jnp.concatenate(ps, axis=0).astype(bf16), seq[2][N_KV + g].astype(bf16),
                     preferred_element_type=f32) for ps, (seq, g) in zip(p_c, units)]
    selm = _select_blocks(jnp.concatenate([ps[0] + ps[1] + ps[2] + ps[3] for ps in p_c], axis=0),
                          jnp.concatenate([qpos] * len(units), axis=0), len(units) * tb)

    ss_slc = scores([seq[3][g, 0:slc_rows, :] for seq, g in units])
    kpos = win_pos0 + lax.broadcasted_iota(jnp.int32, (1, WIN_KEYS), 1)
    d = qpos - kpos
    bias_w = jnp.where((d >= 0) & (d <= WINDOW) & (kpos >= 0), 0.0, NEG)
    o_win = attend(scores([seq[5][g, pl.ds(win_row0, WIN_KEYS), :] for seq, g in units]),
                   [seq[6][g, pl.ds(win_row0, WIN_KEYS), :] for seq, g in units], [bias_w] * len(units))

    kpos = lax.broadcasted_iota(jnp.int32, (1, slc_rows), 1)
    kblk = kpos >> slc_shift
    hot = lax.broadcasted_iota(jnp.int32, (n_cmp, slc_rows), 0) == kblk * (SLC_BLOCK // CMP_STRIDE)
    chosen = jnp.dot(selm, jnp.where(hot, 1.0, 0.0).astype(bf16), preferred_element_type=f32)
    local = (kblk == cur) & (qpos - kpos >= 0)
    o_slc = attend(ss_slc, [seq[4][g, 0:slc_rows, :] for seq, g in units],
                   [jnp.where((chosen[u * tb:(u + 1) * tb] > 0.5) | local, 0.0, NEG) for u in range(len(units))])

    outs = []
    for si, seq in enumerate(seqs):
        sg = seq[1]
        cols = []
        for g in range(N_KV):
            u = si * N_KV + g
            for p in range(HPG):
                h = g * HPG + p
                cols.append(sg[:, h:h + 1] * hs(o_cmp[u], p) + sg[:, N_HEADS + h:N_HEADS + h + 1] * hs(o_slc[u], p)
                            + sg[:, 2 * N_HEADS + h:2 * N_HEADS + h + 1] * hs(o_win[u], p))
        outs.append(jnp.concatenate(cols, axis=1))
    return outs


def _attn_scratch(slc_rows, win_rows):
    return [pltpu.VMEM((N_KV, rows, 2 * HEAD_DIM), bf16) for rows in (slc_rows, slc_rows, win_rows, win_rows)]


def _attn_prompt_kernel(q_ref, kvs_ref, kvw_ref, gn_ref, kc_ref, *refs, T, sb):
    o_ref, sk, sv, wk, wv = refs[-5:]
    tb = WIN_Q_BLOCK
    i = sb * (SLC_CHUNK // tb) + pl.program_id(1)

    @pl.when(pl.program_id(1) == 0)
    def _():
        for g in range(N_KV):
            for src, kdst, vdst in ((kvs_ref, sk, sv), (kvw_ref, wk, wv)):
                kdst[g] = _keys_with_pos(src[:, g * HEAD_DIM:(g + 1) * HEAD_DIM])
                vdst[g] = _vals_with_ones(src[:, (N_KV + g) * HEAD_DIM:(N_KV + g + 1) * HEAD_DIM])

    qpos = i * tb + lax.broadcasted_iota(jnp.int32, (tb, 1), 0)
    win0 = pl.multiple_of(jnp.clip(i * tb - WINDOW, 0, T - WIN_KEYS), LANES)
    o_ref[...] = _attn_core(
        [(q_ref[...], jax.nn.sigmoid(gn_ref[:, 0:LANES]), kc_ref, sk, sv, wk, wv)], qpos, tb=tb,
        slc_rows=(sb + 1) * SLC_CHUNK, win_row0=win0, win_pos0=win0)[0]


def _attn_prompt(z3, kc, sb, o_prev):
    B, T, _ = z3.shape
    assert T % SLC_CHUNK == 0 and T >= WIN_KEYS
    tb = WIN_Q_BLOCK
    nq = SLC_CHUNK // tb
    in_specs = [
        pl.BlockSpec((None, tb, D_ATT), lambda b, i: (b, sb * nq + i, COL_Q // D_ATT)),
        pl.BlockSpec((None, T, KV_W), lambda b, i: (b, 0, COL_KVS // KV_W)),
        pl.BlockSpec((None, T, KV_W), lambda b, i: (b, 0, COL_KVW // KV_W)),
        pl.BlockSpec((None, tb, GN_PAD), lambda b, i: (b, sb * nq + i, COL_GN // GN_PAD)),
        pl.BlockSpec((None, 2 * N_KV, T // CMP_STRIDE, HEAD_DIM), lambda b, i: (b, 0, 0, 0)),
    ]
    args = [z3, z3, z3, z3, kc]
    aliases = {}
    if o_prev is not None:
        aliases = {len(args): 0}
        in_specs.append(pl.BlockSpec(memory_space=pl.ANY))
        args.append(o_prev)
    return pl.pallas_call(
        functools.partial(_attn_prompt_kernel, T=T, sb=sb),
        grid=(B, nq),
        in_specs=in_specs,
        out_specs=pl.BlockSpec((None, tb, D_ATT), lambda b, i: (b, sb * nq + i, 0)),
        out_shape=jax.ShapeDtypeStruct((B, T, D_ATT), f32),
        input_output_aliases=aliases,
        scratch_shapes=_attn_scratch(T, T),
        compiler_params=_params("parallel", "arbitrary"),
        name=f"attn_prompt_{sb}",
    )(*args)


def _attn_sample_kernel(*refs, n_seq, n_pages, page, tq, tb):
    zq_ref, kc_ref, win_ref = refs[1:4]
    page_refs = refs[4:4 + n_seq * n_pages]
    o_ref, skw_ref = refs[4 + n_seq * n_pages:6 + n_seq * n_pages]
    scratch = refs[6 + n_seq * n_pages:]
    past = n_pages * page
    wbuf = win_ref.shape[1] // (2 * N_KV)
    seqs = [_stage_sample_seq(zq_ref.at[s], kc_ref.at[s], win_ref.at[s], page_refs[s * n_pages:(s + 1) * n_pages],
                              skw_ref.at[s], *scratch[4 * s:4 * s + 4], page=page, tq=tq, tb=tb)
            for s in range(n_seq)]
    qpos = past + lax.broadcasted_iota(jnp.int32, (tb, 1), 0)
    outs = _attn_core(seqs, qpos, tb=tb, slc_rows=past + LANES, win_row0=0, win_pos0=past - wbuf)
    for s in range(n_seq):
        o_ref[s] = outs[s]


def _stage_sample_seq(zq_ref, kc_ref, win_ref, page_refs, skw_ref, sk, sv, wk, wv, *, page, tq, tb):
    n_pages = len(page_refs)
    per_tok = 2 * N_KV
    past = n_pages * page
    wbuf = win_ref.shape[0] // per_tok
    q0 = 0
    c_kvs = q0 + D_ATT + KV_W
    c_kvw = c_kvs + KV_W
    c_gn = c_kvw + KV_W

    def new_rows(col):
        return jnp.concatenate([zq_ref[:, col:col + HEAD_DIM], jnp.zeros((LANES - tb, HEAD_DIM), f32)],
                               axis=0).astype(bf16)

    for g in range(N_KV):
        for j in range(n_pages):
            rows = slice(j * page, (j + 1) * page)
            sk[g, rows, 0:HEAD_DIM] = page_refs[j][pl.ds(g, page, stride=per_tok), :].astype(bf16)
            sv[g, rows, 0:HEAD_DIM] = page_refs[j][pl.ds(N_KV + g, page, stride=per_tok), :].astype(bf16)
        sk[g, past:past + LANES, 0:HEAD_DIM] = new_rows(c_kvs + g * HEAD_DIM)
        sv[g, past:past + LANES, 0:HEAD_DIM] = new_rows(c_kvs + (N_KV + g) * HEAD_DIM)
        sk[g, :, HEAD_DIM:] = _pos_cols(past + LANES, 1)
        sv[g, :, HEAD_DIM:] = _ones_col(past + LANES)
        wk[g, 0:wbuf, 0:HEAD_DIM] = win_ref[pl.ds(g, wbuf, stride=per_tok), :].astype(bf16)
        wv[g, 0:wbuf, 0:HEAD_DIM] = win_ref[pl.ds(N_KV + g, wbuf, stride=per_tok), :].astype(bf16)
        wk[g, wbuf:wbuf + LANES, 0:HEAD_DIM] = new_rows(c_kvw + g * HEAD_DIM)
        wv[g, wbuf:wbuf + LANES, 0:HEAD_DIM] = new_rows(c_kvw + (N_KV + g) * HEAD_DIM)
        wk[g, :, HEAD_DIM:] = _pos_cols(wbuf + LANES, 1)
        wv[g, :, HEAD_DIM:] = _ones_col(wbuf + LANES)
    skw_ref[0:(wbuf - tq) * per_tok, :] = win_ref[tq * per_tok:wbuf * per_tok, :]
    for t in range(tq):
        for cg in range(per_tok):
            r = (wbuf - tq + t) * per_tok + cg
            skw_ref[r:r + 1, :] = zq_ref[t:t + 1, c_kvw + cg * HEAD_DIM:c_kvw + (cg + 1) * HEAD_DIM]

    return zq_ref[:, q0:q0 + D_ATT], jax.nn.sigmoid(zq_ref[:, c_gn:c_gn + LANES]), kc_ref, sk, sv, wk, wv


def _attn_sample(zq, kc, cache_win, cache_slc, page_table, *, tq):
    DB, tb, zw = zq.shape
    per_tok = 2 * N_KV
    n_pages = page_table.shape[1]
    page = cache_slc.shape[1] // per_tok
    past = n_pages * page
    wbuf = cache_win.shape[1] // per_tok
    n_seq = 2 if DB % 2 == 0 else 1
    assert wbuf + LANES == WIN_KEYS and tq <= tb and tq < CMP_STRIDE and (tq * per_tok) % SUBLANES == 0
    page_specs = [pl.BlockSpec((None, page * per_tok, HEAD_DIM),
                               functools.partial(lambda b, pt, s, j: (pt[b * n_seq + s, j], 0, 0), s=s, j=j))
                  for s in range(n_seq) for j in range(n_pages)]
    grid_spec = pltpu.PrefetchScalarGridSpec(
        num_scalar_prefetch=1,
        grid=(DB // n_seq,),
        in_specs=[
            pl.BlockSpec((n_seq, tb, zw), lambda b, pt: (b, 0, 0)),
            pl.BlockSpec((n_seq, 2 * N_KV, past // CMP_STRIDE, HEAD_DIM), lambda b, pt: (b, 0, 0, 0)),
            pl.BlockSpec((n_seq, wbuf * per_tok, HEAD_DIM), lambda b, pt: (b, 0, 0)),
        ] + page_specs,
        out_specs=[
            pl.BlockSpec((n_seq, tb, D_ATT), lambda b, pt: (b, 0, 0)),
            pl.BlockSpec((n_seq, wbuf * per_tok, HEAD_DIM), lambda b, pt: (b, 0, 0)),
        ],
        scratch_shapes=_attn_scratch(past + LANES, wbuf + LANES) * n_seq,
    )
    return pl.pallas_call(
        functools.partial(_attn_sample_kernel, n_seq=n_seq, n_pages=n_pages, page=page, tq=tq, tb=tb),
        grid_spec=grid_spec,
        out_shape=[jax.ShapeDtypeStruct((DB, tb, D_ATT), f32),
                   jax.ShapeDtypeStruct((DB, wbuf * per_tok, HEAD_DIM), f32)],
        compiler_params=_params("parallel"),
        name="attn_sample",
    )(page_table, zq, kc, cache_win, *([cache_slc] * (n_seq * n_pages)))


def _merge_kernel(x_ref, yr_ref, o_ref, gm1_ref, gm2_ref, wr_ref, wa_ref, wo_ref, out_ref):
    m = (jax.nn.sigmoid(gm1_ref[...]) * jnp.dot(yr_ref[...].astype(bf16), wr_ref[...], preferred_element_type=f32)
         + jax.nn.sigmoid(gm2_ref[...]) * jnp.dot(o_ref[...].astype(bf16), wa_ref[...], preferred_element_type=f32))
    out_ref[...] = x_ref[...] + jnp.dot(m.astype(bf16), wo_ref[...], preferred_element_type=f32)


def _merge(x2d, yr, o, z2d, wr, wa, wo):
    rows = x2d.shape[0]
    tm = min(256, rows)
    const = lambda shape: pl.BlockSpec(shape, lambda i: (0, 0), pipeline_mode=pl.Buffered(1))
    return pl.pallas_call(
        _merge_kernel,
        grid=(rows // tm,),
        in_specs=[
            pl.BlockSpec((tm, D_MODEL), lambda i: (i, 0)),
            pl.BlockSpec((tm, D_RNN), lambda i: (i, 0)),
            pl.BlockSpec((tm, D_ATT), lambda i: (i, 0)),
            pl.BlockSpec((tm, D_MODEL), lambda i: (i, COL_GM // D_MODEL)),
            pl.BlockSpec((tm, D_MODEL), lambda i: (i, COL_GM // D_MODEL + 1)),
            const((D_RNN, D_MODEL)), const((D_ATT, D_MODEL)), const((D_MODEL, D_MODEL)),
        ],
        out_specs=pl.BlockSpec((tm, D_MODEL), lambda i: (i, 0)),
        out_shape=jax.ShapeDtypeStruct((rows, D_MODEL), f32),
        compiler_params=_params("parallel"),
        name="merge",
    )(x2d, yr, o, z2d, z2d, wr, wa, wo)


def _ffn_kernel(x_ref, prev_ref, gf_ref, wg_ref, wu_ref, wd_ref, cw_ref, cb_ref, gl_ref,
                y_ref, fc_ref, h_ref, acc_ref, gcat, carry, *, S, rows, final_norm):
    ti, fi = pl.program_id(1), pl.program_id(2)
    k1 = FFN_CONV - 1
    pad = _round_up(k1 * S, SUBLANES)

    @pl.when(fi == 0)
    def _():
        h_ref[...] = _rms(x_ref[...], gf_ref[...]).astype(bf16)
        acc_ref[...] = jnp.zeros_like(acc_ref)

    @pl.when(ti == 0)
    def _():
        carry[fi, pad - k1 * S:pad, :] = prev_ref[...]

    h = h_ref[...]
    gate = jnp.dot(h, wg_ref[...], preferred_element_type=f32)
    gcat[pad - k1 * S:pad, :] = carry[fi, pad - k1 * S:pad, :]
    gcat[pad:pad + rows, :] = gate
    u = cb_ref[...] + cw_ref[k1:k1 + 1, :] * gate
    for j in range(k1):
        off = pad - (k1 - j) * S
        u = u + cw_ref[j:j + 1, :] * gcat[off:off + rows, :]
    tail = gcat[pad + rows - k1 * S:pad + rows, :]
    carry[fi, pad - k1 * S:pad, :] = tail
    fc_ref[...] = tail
    up = jnp.dot(h, wu_ref[...], preferred_element_type=f32)
    acc_ref[...] += jnp.dot((_gelu(u) * up).astype(bf16), wd_ref[...], preferred_element_type=f32)

    @pl.when(fi == pl.num_programs(2) - 1)
    def _():
        out = x_ref[...] + acc_ref[...]
        y_ref[...] = _rms(out, gl_ref[...]) if final_norm else out


def _ffn(x3, prev, g_ffn, wg, wu, wd, cw, cb, g_fin, *, S, Tt, final_norm):
    nb, rper, _ = x3.shape
    rows = Tt * S
    tf = 768
    nf = D_FF // tf
    k1 = FFN_CONV - 1
    pad = _round_up(k1 * S, SUBLANES)
    assert rper % rows == 0 and Tt >= k1
    return pl.pallas_call(
        functools.partial(_ffn_kernel, S=S, rows=rows, final_norm=final_norm),
        grid=(nb, rper // rows, nf),
        in_specs=[
            pl.BlockSpec((None, rows, D_MODEL), lambda b, t, f: (b, t, 0)),
            pl.BlockSpec((None, k1 * S, tf), lambda b, t, f: (b, 0, f)),
            pl.BlockSpec((1, D_MODEL), lambda b, t, f: (0, 0)),
            pl.BlockSpec((D_MODEL, tf), lambda b, t, f: (0, f)),
            pl.BlockSpec((D_MODEL, tf), lambda b, t, f: (0, f)),
            pl.BlockSpec((tf, D_MODEL), lambda b, t, f: (f, 0)),
            pl.BlockSpec((FFN_CONV, tf), lambda b, t, f: (0, f)),
            pl.BlockSpec((1, tf), lambda b, t, f: (0, f)),
            pl.BlockSpec((1, D_MODEL), lambda b, t, f: (0, 0)),
        ],
        out_specs=[
            pl.BlockSpec((None, rows, D_MODEL), lambda b, t, f: (b, t, 0)),
            pl.BlockSpec((None, None, k1 * S, tf), lambda b, t, f: (b, t, 0, f)),
        ],
        out_shape=[
            jax.ShapeDtypeStruct((nb, rper, D_MODEL), f32),
            jax.ShapeDtypeStruct((nb, rper // rows, k1 * S, D_FF), f32),
        ],
        scratch_shapes=[
            pltpu.VMEM((rows, D_MODEL), bf16),
            pltpu.VMEM((rows, D_MODEL), f32),
            pltpu.VMEM((pad + rows, tf), f32),
            pltpu.VMEM((nf, pad, tf), f32),
        ],
        compiler_params=_params("parallel", "arbitrary", "arbitrary"),
        name="ffn",
    )(x3, prev, g_ffn, wg, wu, wd, cw, cb, g_fin)


def _time_major(a):
    return jnp.swapaxes(a, 0, 1).reshape(1, a.shape[0] * a.shape[1], a.shape[2])


def _batch_major(a, db):
    return jnp.swapaxes(a.reshape(a.shape[1] // db, db, a.shape[2]), 0, 1)


def kernel(x_prompt, x_sample, cache_kv_cmp, cache_kv_slc, cache_kv_win, state_rnn_h, state_rnn_conv, state_ffn_conv, page_table, norm_mix, w_in, rnn_conv_w, rnn_conv_b, rnn_wa, rnn_ba, rnn_wx, rnn_bx, rnn_lambda, cmp_w, cmp_pos, w_proj_rnn, w_proj_att, w_out, norm_ffn, ffn_w_gate, ffn_w_up, ffn_conv_w, ffn_conv_b, ffn_w_down, norm_final):
    B, T, _ = x_prompt.shape
    DB, Tq, _ = x_sample.shape
    depth = w_in.shape[0]
    n_pool, page = cache_kv_cmp.shape[1:3]
    n_pages = page_table.shape[1]
    past = n_pages * page
    wbuf = cache_kv_win.shape[2]
    tb = SUBLANES
    assert page % CMP_STRIDE == 0 and page % SLC_BLOCK == 0 and Tq >= RNN_CONV - 1 and Tq <= tb
    assert Tq & (Tq - 1) == 0 and DB % SUBLANES == 0 and T % 256 == 0
    kv_shape = (2, N_KV, HEAD_DIM)

    xp = x_prompt.reshape(B * T, D_MODEL)
    xs = _time_major(x_sample)[0]
    pst = [[] for _ in range(6)]
    sst = [[] for _ in range(6)]
    g_out = norm_final.reshape(1, D_MODEL)
    for l in range(depth):
        w = w_in[l]
        widths = (D_RNN, D_RNN, D_ATT, KV_W, KV_W, KV_W, 3 * N_HEADS, 2 * D_MODEL)
        cuts = [0]
        for wd_ in widths:
            cuts.append(cuts[-1] + wd_)
        part = lambda k: w[:, cuts[k]:cuts[k + 1]]
        w_cat = jnp.concatenate(
            [part(0), part(1), part(7), part(2), part(3), part(4), part(5),
             jnp.pad(part(6), ((0, 0), (0, GN_PAD - 3 * N_HEADS)))], axis=1).astype(bf16)
        g_mix = norm_mix[l].reshape(1, D_MODEL)
        cw, cb = rnn_conv_w[l], rnn_conv_b[l].reshape(1, D_RNN)
        wgate = jnp.concatenate([rnn_wa[l], rnn_wx[l]], axis=-1).astype(bf16)
        ba, bx, lam = (v[l].reshape(1, D_RNN) for v in (rnn_ba, rnn_bx, rnn_lambda))
        cmpw = cmp_w[l].astype(bf16)
        wc = jnp.concatenate([cmpw[:, :CMP_STRIDE].reshape(2, CMP_STRIDE * HEAD_DIM, HEAD_DIM),
                              cmpw[:, CMP_STRIDE:].reshape(2, CMP_STRIDE * HEAD_DIM, HEAD_DIM)], axis=-1)
        pw = cmpw.reshape(2, CMP_BLOCK * HEAD_DIM, HEAD_DIM)
        pos = cmp_pos[l].reshape(2, 1, CMP_BLOCK * HEAD_DIM)
        wr, wa, wo = w_proj_rnn[l].astype(bf16), w_proj_att[l].astype(bf16), w_out[l].astype(bf16)
        g_ffn = norm_ffn[l].reshape(1, D_MODEL)
        fwg, fwu, fwd = ffn_w_gate[l].astype(bf16), ffn_w_up[l].astype(bf16), ffn_w_down[l].astype(bf16)
        fcw, fcb = ffn_conv_w[l], ffn_conv_b[l].reshape(1, D_FF)
        last = l == depth - 1

        per_tok = 2 * N_KV
        z, kvc_rows, kvs_rows, kvw_rows = _inproj(xp, g_mix, w_cat)
        z3 = z.reshape(B, T, D_Z)
        y_rnn, h_last, rconv = _rglru(z3, jnp.zeros((B, RNN_CONV - 1, D_RNN), f32), jnp.zeros((B, 1, D_RNN), f32),
                                      cw, cb, wgate, ba, bx, lam, S=1, Tt=256)
        kc = _kc_call(kvc_rows.reshape(B, T * per_tok, HEAD_DIM), 1, B, wc, pw, pos)
        o = None
        for sb in range(T // SLC_CHUNK):
            o = _attn_prompt(z3, kc, sb, o)
        x1 = _merge(xp, y_rnn.reshape(B * T, D_RNN), o.reshape(B * T, D_ATT), z, wr, wa, wo)
        y, fconv = _ffn(x1.reshape(B, T, D_MODEL), jnp.zeros((B, FFN_CONV - 1, D_FF), f32), g_ffn, fwg, fwu, fwd,
                        fcw, fcb, g_out, S=1, Tt=512, final_norm=last)
        xp = y.reshape(B * T, D_MODEL)
        wlen = min(WINDOW, T)
        for lst, v in zip(pst, (kvc_rows.reshape(B, T, *kv_shape), kvs_rows.reshape(B, T, *kv_shape),
                                kvw_rows.reshape(B, T, *kv_shape)[:, T - wlen:], h_last[:, 0], rconv, fconv[:, -1])):
            lst.append(v)

        zs = _inproj(xs, g_mix, w_cat)[0]
        zs3 = zs.reshape(1, Tq * DB, D_Z)
        ys_rnn, hs_last, sconv = _rglru(zs3, _time_major(state_rnn_conv[l]), state_rnn_h[l][None],
                                        cw, cb, wgate, ba, bx, lam, S=DB, Tt=Tq)
        zq = _batch_major(zs3[:, :, COL_Q:], DB)
        zq = jnp.pad(zq, ((0, 0), (0, tb - Tq), (0, 0)))
        kcs = _kc_call(cache_kv_cmp[l].reshape(n_pool, page * per_tok, HEAD_DIM), n_pages, DB, wc, pw, pos,
                       page_table=page_table)
        os_, s_win = _attn_sample(zq, kcs, cache_kv_win[l].reshape(DB, wbuf * per_tok, HEAD_DIM),
                                  cache_kv_slc[l].reshape(n_pool, page * per_tok, HEAD_DIM), page_table, tq=Tq)
        os_tm = _time_major(os_[:, :Tq])[0]
        xs1 = _merge(xs, ys_rnn[0], os_tm, zs, wr, wa, wo)
        ys, sfconv = _ffn(xs1[None], _time_major(state_ffn_conv[l]), g_ffn, fwg, fwu, fwd, fcw, fcb, g_out,
                          S=DB, Tt=Tq, final_norm=last)
        xs = ys[0]
        c0 = D_ATT
        for lst, v in zip(sst, (zq[:, :Tq, c0:c0 + KV_W].reshape(DB, Tq, *kv_shape),
                                zq[:, :Tq, c0 + KV_W:c0 + 2 * KV_W].reshape(DB, Tq, *kv_shape),
                                s_win.reshape(DB, wbuf, *kv_shape), hs_last[0],
                                _batch_major(sconv, DB), _batch_major(sfconv[:, -1], DB))):
            lst.append(v)

    y_prompt = xp.reshape(B, T, D_MODEL)
    y_sample = _batch_major(xs[None], DB)
    return (y_prompt, y_sample, *[jnp.stack(v) for v in pst], *[jnp.stack(v) for v in sst])
```

```python
import functools
import math

import jax
import jax.numpy as jnp
from jax import lax
from jax.experimental import pallas as pl
from jax.experimental.pallas import tpu as pltpu

D_MODEL = 2048
D_RNN = D_MODEL // 2
RNN_BLOCKS = 8
RNN_BLK = D_RNN // RNN_BLOCKS
RNN_CONV = 4
LRU_C = 8.0
HEAD_DIM = 128
D_ATT = D_MODEL // 2
N_HEADS = D_ATT // HEAD_DIM
N_KV = 2
HPG = N_HEADS // N_KV
KV_W = 2 * N_KV * HEAD_DIM
CMP_BLOCK = 32
CMP_STRIDE = 16
SLC_BLOCK = 64
TOPK = 8
WINDOW = 512
WIN_Q_BLOCK = 128
D_FF = 3 * D_MODEL
FFN_CONV = 3
RMS_EPS = 1e-6
NEG = -1e30

COL_XR = 0
COL_GR = COL_XR + D_RNN
COL_GM = COL_GR + D_RNN
COL_Q = COL_GM + 2 * D_MODEL
COL_KVC = COL_Q + D_ATT
COL_KVS = COL_KVC + KV_W
COL_KVW = COL_KVS + KV_W
COL_GN = COL_KVW + KV_W
GN_PAD = KV_W
D_Z = COL_GN + GN_PAD

VMEM_LIMIT_BYTES = 56 * 1024 * 1024
SUBLANES = 8
LANES = 128

WIN_KEYS = WINDOW + WIN_Q_BLOCK
KC_PITCH = CMP_STRIDE * 2 * N_KV + SUBLANES
SLC_CHUNK = 512

f32 = jnp.float32
bf16 = jnp.bfloat16


def _round_up(x, m):
    return (x + m - 1) // m * m


def _params(*sem):
    return pltpu.CompilerParams(dimension_semantics=sem, vmem_limit_bytes=VMEM_LIMIT_BYTES)


def _rms(x, g):
    return (x * lax.rsqrt(jnp.mean(x * x, axis=-1, keepdims=True) + RMS_EPS)) * g


def _gelu(x):
    return x * (0.5 * (1.0 + jnp.tanh(math.sqrt(2.0 / math.pi) * (x + 0.044715 * (x * x * x)))))


def _nt(a, b):
    return lax.dot_general(a, b, (((1,), (1,)), ((), ())), preferred_element_type=f32)


def _inproj_kernel(x_ref, g_ref, w_ref, o_ref, *refs, tm, tn):
    kv_refs, h_ref = refs[:3], refs[3]

    @pl.when(pl.program_id(1) == 0)
    def _():
        h_ref[...] = _rms(x_ref[...], g_ref[...]).astype(bf16)

    o_ref[...] = jnp.dot(h_ref[...], w_ref[...], preferred_element_type=f32)

    per_tok = 2 * N_KV
    for col, kv_ref in zip((COL_KVC, COL_KVS, COL_KVW), kv_refs):
        lo = col % tn

        @pl.when(pl.program_id(1) == col // tn)
        def _(lo=lo, kv_ref=kv_ref):
            for cg in range(per_tok):
                kv_ref[pl.ds(cg, tm, stride=per_tok), :] = o_ref[:, lo + cg * HEAD_DIM:lo + (cg + 1) * HEAD_DIM]


def _inproj(x2d, g, w_cat):
    rows = x2d.shape[0]
    tm = min(1024, rows)
    tn = 1536
    per_tok = 2 * N_KV
    assert rows % tm == 0 and D_Z % tn == 0
    assert all(c % tn + KV_W <= tn for c in (COL_KVC, COL_KVS, COL_KVW))
    kv_spec = pl.BlockSpec((tm * per_tok, HEAD_DIM), lambda i, j: (i, 0), pipeline_mode=pl.Buffered(1))
    kv_shape = jax.ShapeDtypeStruct((rows * per_tok, HEAD_DIM), f32)
    return pl.pallas_call(
        functools.partial(_inproj_kernel, tm=tm, tn=tn),
        grid=(rows // tm, D_Z // tn),
        in_specs=[
            pl.BlockSpec((tm, D_MODEL), lambda i, j: (i, 0)),
            pl.BlockSpec((1, D_MODEL), lambda i, j: (0, 0)),
            pl.BlockSpec((D_MODEL, tn), lambda i, j: (0, j)),
        ],
        out_specs=[pl.BlockSpec((tm, tn), lambda i, j: (i, j)), kv_spec, kv_spec, kv_spec],
        out_shape=[jax.ShapeDtypeStruct((rows, D_Z), f32), kv_shape, kv_shape, kv_shape],
        scratch_shapes=[pltpu.VMEM((tm, D_MODEL), bf16)],
        compiler_params=_params("parallel", "arbitrary"),
        name="inproj",
    )(x2d, g, w_cat)


def _rglru_kernel(xr_ref, gr_ref, prev_ref, h0_ref, cw_ref, cb_ref, wg_ref, ba_ref, bx_ref, lam_ref,
                  y_ref, hl_ref, nc_ref, xt, gt, ha, hu, hc, *, NB, S, Tt):
    SE = NB * S
    rows = Tt * SE
    k1 = RNN_CONV - 1
    pad = k1 * SE
    slab = lambda n: slice(n * RNN_BLK, (n + 1) * RNN_BLK)

    @pl.when(pl.program_id(1) == 0)
    def _():
        for n in range(RNN_BLOCKS):
            xt[n, 0:pad, :] = prev_ref[:, slab(n)]
            hc[n] = h0_ref[:, slab(n)]

    def time_rows(b):
        return pl.ds(b, Tt, stride=NB) if NB > 1 else pl.ds(0, rows)

    lam = lam_ref[...]
    softplus_neg_lam = jnp.maximum(-lam, 0.0) + jnp.log1p(jnp.exp(-jnp.abs(lam)))
    for n in range(RNN_BLOCKS):
        sl = slab(n)
        for b in range(NB):
            if NB > 1:
                xt[n, pl.ds(pad + b, Tt, stride=NB), :] = xr_ref[b, :, sl]
            else:
                xt[n, pad:pad + rows, :] = xr_ref[b, :, sl]
            gt[n, time_rows(b), :] = gr_ref[b, :, sl]
        xc = cb_ref[:, sl] + cw_ref[k1:k1 + 1, sl] * xt[n, pad:pad + rows, :]
        for j in range(k1):
            off = pad - (k1 - j) * SE
            xc = xc + cw_ref[j:j + 1, sl] * xt[n, off:off + rows, :]
        tail = xt[n, rows:rows + pad, :]
        xt[n, 0:pad, :] = tail
        nc_ref[:, sl] = tail
        gates = jnp.dot(xc.astype(bf16), wg_ref[n], preferred_element_type=f32)
        r = jax.nn.sigmoid(gates[:, :RNN_BLK] + ba_ref[:, sl])
        i = jax.nn.sigmoid(gates[:, RNN_BLK:] + bx_ref[:, sl])
        log_a = (-LRU_C) * r * softplus_neg_lam[:, sl]
        a = jnp.exp(log_a)
        one_minus_a2 = -jnp.tanh(log_a) * (a * a + 1.0)
        ha[n] = a
        hu[n] = jnp.sqrt(one_minus_a2) * (i * xc)

    hs = [hc[n] for n in range(RNN_BLOCKS)]
    for t in range(Tt):
        step = slice(t * SE, (t + 1) * SE)
        for n in range(RNN_BLOCKS):
            hs[n] = ha[n, step, :] * hs[n] + hu[n, step, :]
            hu[n, step, :] = hs[n]

    for n in range(RNN_BLOCKS):
        sl = slab(n)
        hc[n] = hs[n]
        hl_ref[:, sl] = hs[n]
        gt[n] = hu[n] * _gelu(gt[n])
        for b in range(NB):
            y_ref[b, :, sl] = gt[n, time_rows(b), :]


def _rglru(z3, prev, h0, cw, cb, wg, ba, bx, lam, *, NB, S, Tt):
    groups, rper, _ = z3.shape
    SE = NB * S
    rows = Tt * SE
    k1 = RNN_CONV - 1
    pad = k1 * SE
    assert groups % NB == 0 and rper % (Tt * S) == 0 and Tt >= k1 and SE % SUBLANES == 0 and (NB == 1 or S == 1)
    vec = lambda: pl.BlockSpec((1, D_RNN), lambda b, t: (0, 0))
    slabs = lambda r: pltpu.VMEM((RNN_BLOCKS, r, RNN_BLK), f32)
    return pl.pallas_call(
        functools.partial(_rglru_kernel, NB=NB, S=S, Tt=Tt),
        grid=(groups // NB, rper // (Tt * S)),
        in_specs=[
            pl.BlockSpec((NB, Tt * S, D_RNN), lambda b, t: (b, t, COL_XR // D_RNN)),
            pl.BlockSpec((NB, Tt * S, D_RNN), lambda b, t: (b, t, COL_GR // D_RNN)),
            pl.BlockSpec((pad, D_RNN), lambda b, t: (b, 0)),
            pl.BlockSpec((SE, D_RNN), lambda b, t: (b, 0)),
            pl.BlockSpec((RNN_CONV, D_RNN), lambda b, t: (0, 0)),
            vec(),
            pl.BlockSpec((RNN_BLOCKS, RNN_BLK, 2 * RNN_BLK), lambda b, t: (0, 0, 0)),
            vec(), vec(), vec(),
        ],
        out_specs=[
            pl.BlockSpec((NB, Tt * S, D_RNN), lambda b, t: (b, t, 0)),
            pl.BlockSpec((SE, D_RNN), lambda b, t: (b, 0)),
            pl.BlockSpec((pad, D_RNN), lambda b, t: (b, 0)),
        ],
        out_shape=[
            jax.ShapeDtypeStruct((groups, rper, D_RNN), f32),
            jax.ShapeDtypeStruct((groups * S, D_RNN), f32),
            jax.ShapeDtypeStruct((groups // NB * pad, D_RNN), f32),
        ],
        scratch_shapes=[slabs(pad + rows), slabs(rows), slabs(rows), slabs(rows), slabs(SE)],
        compiler_params=_params("parallel", "arbitrary"),
        name="rglru",
    )(z3, z3, prev, h0, cw, cb, wg, ba, bx, lam)


def _kc_kernel(*refs, n_seq, n_pieces):
    refs = refs[len(refs) - (n_seq * n_pieces + 4 + n_seq):]
    x_refs = refs[:n_seq * n_pieces]
    wc_ref, pw_ref, pos_ref, kc_ref = refs[n_seq * n_pieces:n_seq * n_pieces + 4]
    xps = refs[n_seq * n_pieces + 4:]
    per_tok = 2 * N_KV
    ch = CMP_STRIDE * per_tok
    for s in range(n_seq):
        nc = 0
        for r in x_refs[s * n_pieces:(s + 1) * n_pieces]:
            for k in range(r.shape[0] // ch):
                xps[s][nc * KC_PITCH:nc * KC_PITCH + ch, :] = r[k * ch:(k + 1) * ch, :]
                nc += 1
    for c in range(2):
        groups = [jnp.concatenate([xps[s][pl.ds(l * per_tok + c * N_KV + g, nc, stride=KC_PITCH), :]
                                   for l in range(CMP_STRIDE)], axis=1) for s in range(n_seq) for g in range(N_KV)]
        xs = jnp.concatenate(groups, axis=0).astype(bf16)
        fs = jnp.dot(xs, wc_ref[c], preferred_element_type=f32)
        pos = jnp.broadcast_to(pos_ref[c], (SUBLANES, CMP_BLOCK * HEAD_DIM)).astype(bf16)
        pos_bias = jnp.dot(pos, pw_ref[c], preferred_element_type=f32)[0:1, :]
        for s in range(n_seq):
            for g in range(N_KV):
                u = s * N_KV + g
                first = fs[u * nc:(u + 1) * nc, :HEAD_DIM]
                second = fs[u * nc:(u + 1) * nc, HEAD_DIM:]
                kc_ref[s, c * N_KV + g] = first + pltpu.roll(second, nc - 1, 0) + pos_bias


def _kc_call(x, n_pieces, nseq, wc, pw, pos, page_table=None):
    assert page_table is not None or n_pieces == 1
    piece_rows = x.shape[1]
    nc = n_pieces * piece_rows // (CMP_STRIDE * 2 * N_KV)
    n_seq = 2 if nseq % 2 == 0 else 1
    nsp = 0 if page_table is None else 1
    const = lambda shape: pl.BlockSpec(shape, lambda b, *_: (0,) * len(shape))

    def piece_index(b, *pt, s, j):
        return (pt[0][b * n_seq + s, j] if pt else b * n_seq + s, 0, 0)

    grid_spec = pltpu.PrefetchScalarGridSpec(
        num_scalar_prefetch=nsp,
        grid=(nseq // n_seq,),
        in_specs=[pl.BlockSpec((None, piece_rows, HEAD_DIM), functools.partial(piece_index, s=s, j=j))
                  for s in range(n_seq) for j in range(n_pieces)] + [
            const((2, CMP_STRIDE * HEAD_DIM, 2 * HEAD_DIM)),
            const((2, CMP_BLOCK * HEAD_DIM, HEAD_DIM)),
            const((2, 1, CMP_BLOCK * HEAD_DIM)),
        ],
        out_specs=pl.BlockSpec((n_seq, 2 * N_KV, nc, HEAD_DIM), lambda b, *_: (b, 0, 0, 0)),
        scratch_shapes=[pltpu.VMEM((nc * KC_PITCH, HEAD_DIM), f32)] * n_seq,
    )
    args = ([] if page_table is None else [page_table]) + [x] * (n_seq * n_pieces) + [wc, pw, pos]
    return pl.pallas_call(
        functools.partial(_kc_kernel, n_seq=n_seq, n_pieces=n_pieces),
        grid_spec=grid_spec,
        out_shape=jax.ShapeDtypeStruct((nseq, 2 * N_KV, nc, HEAD_DIM), f32),
        compiler_params=_params("parallel"),
        name="kc_sample" if nsp else "kc_prompt",
    )(*args)


def _pos_cols(n_rows, step):
    r = lax.broadcasted_iota(jnp.int32, (n_rows, HEAD_DIM), 0) * step
    lane = lax.broadcasted_iota(jnp.int32, (n_rows, HEAD_DIM), 1)
    low_bits = SLC_BLOCK.bit_length() - 1
    hi = (r >> low_bits) << low_bits
    return jnp.where(lane == 0, hi, jnp.where(lane == 1, r - hi, 0)).astype(f32).astype(bf16)


def _keys_with_pos(k, step=1):
    return jnp.concatenate([k.astype(bf16), _pos_cols(k.shape[0], step)], axis=1)


def _ones_col(n_rows):
    lane = lax.broadcasted_iota(jnp.int32, (n_rows, HEAD_DIM), 1)
    return jnp.where(lane == 0, 1.0, 0.0).astype(bf16)


def _vals_with_ones(v):
    return jnp.concatenate([v.astype(bf16), _ones_col(v.shape[0])], axis=1)


def _exp_rows(s, mask_bias):
    s = s + mask_bias
    m = jnp.max(s, axis=-1, keepdims=True)
    e = jnp.exp(s - m)
    inv = jnp.where(m > 0.5 * NEG, 1.0 / jnp.maximum(jnp.sum(e, axis=-1, keepdims=True), 1e-30), 0.0)
    return e, inv


def _select_blocks(psum, qpos, tb):
    lanes_per_blk = SLC_BLOCK // CMP_STRIDE
    n_lane = psum.shape[1]
    imp = psum + pltpu.roll(psum, 1, 1)
    for back in range(1, lanes_per_blk):
        imp = imp + pltpu.roll(psum, n_lane - back, 1)
    lane = lax.broadcasted_iota(jnp.int32, (tb, n_lane), 1)
    lane_f = lane.astype(f32)
    blk = lane >> (lanes_per_blk.bit_length() - 1)
    cur = qpos >> (SLC_BLOCK.bit_length() - 1)
    cand = ((lane & (lanes_per_blk - 1)) == 0) & (blk >= 1) & (blk < cur)
    score = jnp.where(cand, imp, NEG)
    sel = (lane == 0) & (cur > 0)
    for _ in range(TOPK - 2):
        best = jnp.max(score, axis=-1, keepdims=True)
        first = jnp.min(jnp.where(score == best, lane_f, float(n_lane)), axis=-1, keepdims=True)
        pick = (lane_f == first) & (best > 0.5 * NEG)
        sel = sel | pick
        score = jnp.where(pick, NEG, score)
    return jnp.where(sel, 1.0, 0.0).astype(bf16)


def _attn_core(seqs, qpos, *, tb, slc_rows, win_row0, win_pos0):
    scale = 1.0 / math.sqrt(HEAD_DIM)
    units = [(seq, g) for seq in seqs for g in range(N_KV)]
    n_cmp = seqs[0][2].shape[1]
    slc_shift = SLC_BLOCK.bit_length() - 1
    cur = qpos >> slc_shift
    lane_c = lax.broadcasted_iota(jnp.int32, (tb, n_cmp), 1)
    bias_c = jnp.where(qpos - (CMP_STRIDE * lane_c + (CMP_BLOCK - 1)) >= 0, 0.0, NEG)
    lane_q = lax.broadcasted_iota(jnp.int32, (tb, HEAD_DIM), 1)
    hs = lambda x, p: x[p * tb:(p + 1) * tb]

    qgs = []
    for (q, *_), g in units:
        parts = []
        for p in range(HPG):
            h = g * HPG + p
            slope = 2.0 ** (-8.0 * (h + 1) / N_HEADS)
            parts.append(jnp.concatenate([q[:, h * HEAD_DIM:(h + 1) * HEAD_DIM] * scale,
                                          jnp.where(lane_q < 2, slope, 0.0)], axis=1))
        qgs.append(jnp.concatenate(parts, axis=0).astype(bf16))

    def scores(keys):
        return [_nt(qg, k) for qg, k in zip(qgs, keys)]

    def attend(ss, vals, biases):
        sps = [[hs(s, p) + b for p in range(HPG)] for s, b in zip(ss, biases)]
        ms = [[jnp.max(sp, axis=-1, keepdims=True) for sp in row] for row in sps]
        es = [jnp.concatenate([jnp.exp(sp - m) for sp, m in zip(row, mrow)], axis=0).astype(bf16)
              for row, mrow in zip(sps, ms)]
        pvs = [jnp.dot(e, v, preferred_element_type=f32) for e, v in zip(es, vals)]
        return [pv[:, :HEAD_DIM] / jnp.maximum(pv[:, HEAD_DIM:HEAD_DIM + 1], 1e-30) for pv in pvs]

    ss = [_nt(qg, _keys_with_pos(seq[2][g], CMP_STRIDE)) for qg, (seq, g) in zip(qgs, units)]
    p_c = []
    for s in ss:
        ps = []
        for p in range(HPG):
            e, inv = _exp_rows(hs(s, p), bias_c)
            ps.append(e * inv)
        p_c.append(ps)
    o_cmp = [jnp.dot(jnp.concatenate(ps, axis=0).astype(bf16), seq[2][N_KV + g].astype(bf16),
                     preferred_element_type=f32) for ps, (seq, g) in zip(p_c, units)]
    selm = _select_blocks(jnp.concatenate([ps[0] + ps[1] + ps[2] + ps[3] for ps in p_c], axis=0),
                          jnp.concatenate([qpos] * len(units), axis=0), len(units) * tb)

    ss_slc = scores([seq[3][g, 0:slc_rows, :] for seq, g in units])
    kpos = win_pos0 + lax.broadcasted_iota(jnp.int32, (1, WIN_KEYS), 1)
    d = qpos - kpos
    bias_w = jnp.where((d >= 0) & (d <= WINDOW) & (kpos >= 0), 0.0, NEG)
    o_win = attend(scores([seq[5][g, pl.ds(win_row0, WIN_KEYS), :] for seq, g in units]),
                   [seq[6][g, pl.ds(win_row0, WIN_KEYS), :] for seq, g in units], [bias_w] * len(units))

    kpos = lax.broadcasted_iota(jnp.int32, (1, slc_rows), 1)
    kblk = kpos >> slc_shift
    hot = lax.broadcasted_iota(jnp.int32, (n_cmp, slc_rows), 0) == kblk * (SLC_BLOCK // CMP_STRIDE)
    chosen = jnp.dot(selm, jnp.where(hot, 1.0, 0.0).astype(bf16), preferred_element_type=f32)
    local = (kblk == cur) & (qpos - kpos >= 0)
    o_slc = attend(ss_slc, [seq[4][g, 0:slc_rows, :] for seq, g in units],
                   [jnp.where((chosen[u * tb:(u + 1) * tb] > 0.5) | local, 0.0, NEG) for u in range(len(units))])

    outs = []
    for si, seq in enumerate(seqs):
        sg = seq[1]
        cols = []
        for g in range(N_KV):
            u = si * N_KV + g
            for p in range(HPG):
                h = g * HPG + p
                cols.append(sg[:, h:h + 1] * hs(o_cmp[u], p) + sg[:, N_HEADS + h:N_HEADS + h + 1] * hs(o_slc[u], p)
                            + sg[:, 2 * N_HEADS + h:2 * N_HEADS + h + 1] * hs(o_win[u], p))
        outs.append(jnp.concatenate(cols, axis=1))
    return outs


def _attn_scratch(slc_rows, win_rows):
    return [pltpu.VMEM((N_KV, rows, 2 * HEAD_DIM), bf16) for rows in (slc_rows, slc_rows, win_rows, win_rows)]


def _attn_prompt_kernel(q_ref, kvs_ref, kvw_ref, gn_ref, kc_ref, *refs, T, sb):
    o_ref, sk, sv, wk, wv = refs[-5:]
    tb = WIN_Q_BLOCK
    i = sb * (SLC_CHUNK // tb) + pl.program_id(1)

    @pl.when(pl.program_id(1) == 0)
    def _():
        for g in range(N_KV):
            for src, kdst, vdst in ((kvs_ref, sk, sv), (kvw_ref, wk, wv)):
                kdst[g] = _keys_with_pos(src[:, g * HEAD_DIM:(g + 1) * HEAD_DIM])
                vdst[g] = _vals_with_ones(src[:, (N_KV + g) * HEAD_DIM:(N_KV + g + 1) * HEAD_DIM])

    qpos = i * tb + lax.broadcasted_iota(jnp.int32, (tb, 1), 0)
    win0 = pl.multiple_of(jnp.clip(i * tb - WINDOW, 0, T - WIN_KEYS), LANES)
    o_ref[...] = _attn_core(
        [(q_ref[...], jax.nn.sigmoid(gn_ref[:, 0:LANES]), kc_ref, sk, sv, wk, wv)], qpos, tb=tb,
        slc_rows=(sb + 1) * SLC_CHUNK, win_row0=win0, win_pos0=win0)[0]


def _attn_prompt(z3, kc, sb, o_prev):
    B, T, _ = z3.shape
    assert T % SLC_CHUNK == 0 and T >= WIN_KEYS
    tb = WIN_Q_BLOCK
    nq = SLC_CHUNK // tb
    in_specs = [
        pl.BlockSpec((None, tb, D_ATT), lambda b, i: (b, sb * nq + i, COL_Q // D_ATT)),
        pl.BlockSpec((None, T, KV_W), lambda b, i: (b, 0, COL_KVS // KV_W)),
        pl.BlockSpec((None, T, KV_W), lambda b, i: (b, 0, COL_KVW // KV_W)),
        pl.BlockSpec((None, tb, GN_PAD), lambda b, i: (b, sb * nq + i, COL_GN // GN_PAD)),
        pl.BlockSpec((None, 2 * N_KV, T // CMP_STRIDE, HEAD_DIM), lambda b, i: (b, 0, 0, 0)),
    ]
    in_specs.append(pl.BlockSpec(memory_space=pl.ANY))
    return pl.pallas_call(
        functools.partial(_attn_prompt_kernel, T=T, sb=sb),
        grid=(B, nq),
        in_specs=in_specs,
        out_specs=pl.BlockSpec((None, tb, D_ATT), lambda b, i: (b, sb * nq + i, 0)),
        out_shape=jax.ShapeDtypeStruct((B, T, D_ATT), f32),
        input_output_aliases={len(in_specs) - 1: 0},
        scratch_shapes=_attn_scratch(T, T),
        compiler_params=_params("parallel", "arbitrary"),
        name=f"attn_prompt_{sb}",
    )(z3, z3, z3, z3, kc, o_prev)


def _attn_sample_kernel(*refs, n_seq, n_pages, page, tq, tb):
    zq_ref, kc_ref, win_ref = refs[1:4]
    page_refs = refs[4:4 + n_seq * n_pages]
    o_ref, skw_ref = refs[4 + n_seq * n_pages:6 + n_seq * n_pages]
    scratch = refs[6 + n_seq * n_pages:]
    past = n_pages * page
    wbuf = win_ref.shape[1] // (2 * N_KV)
    seqs = [_stage_sample_seq(zq_ref.at[s], kc_ref.at[s], win_ref.at[s], page_refs[s * n_pages:(s + 1) * n_pages],
                              skw_ref.at[s], *scratch[4 * s:4 * s + 4], page=page, tq=tq, tb=tb)
            for s in range(n_seq)]
    qpos = past + lax.broadcasted_iota(jnp.int32, (tb, 1), 0)
    outs = _attn_core(seqs, qpos, tb=tb, slc_rows=past + LANES, win_row0=0, win_pos0=past - wbuf)
    for s in range(n_seq):
        o_ref[s] = outs[s]


def _stage_sample_seq(zq_ref, kc_ref, win_ref, page_refs, skw_ref, sk, sv, wk, wv, *, page, tq, tb):
    n_pages = len(page_refs)
    per_tok = 2 * N_KV
    past = n_pages * page
    wbuf = win_ref.shape[0] // per_tok
    q0 = 0
    c_kvs = q0 + D_ATT + KV_W
    c_kvw = c_kvs + KV_W
    c_gn = c_kvw + KV_W

    def new_rows(col):
        return jnp.concatenate([zq_ref[:, col:col + HEAD_DIM], jnp.zeros((LANES - tb, HEAD_DIM), f32)],
                               axis=0).astype(bf16)

    for g in range(N_KV):
        for j in range(n_pages):
            rows = slice(j * page, (j + 1) * page)
            sk[g, rows, 0:HEAD_DIM] = page_refs[j][pl.ds(g, page, stride=per_tok), :].astype(bf16)
            sv[g, rows, 0:HEAD_DIM] = page_refs[j][pl.ds(N_KV + g, page, stride=per_tok), :].astype(bf16)
        sk[g, past:past + LANES, 0:HEAD_DIM] = new_rows(c_kvs + g * HEAD_DIM)
        sv[g, past:past + LANES, 0:HEAD_DIM] = new_rows(c_kvs + (N_KV + g) * HEAD_DIM)
        sk[g, :, HEAD_DIM:] = _pos_cols(past + LANES, 1)
        sv[g, :, HEAD_DIM:] = _ones_col(past + LANES)
        wk[g, 0:wbuf, 0:HEAD_DIM] = win_ref[pl.ds(g, wbuf, stride=per_tok), :].astype(bf16)
        wv[g, 0:wbuf, 0:HEAD_DIM] = win_ref[pl.ds(N_KV + g, wbuf, stride=per_tok), :].astype(bf16)
        wk[g, wbuf:wbuf + LANES, 0:HEAD_DIM] = new_rows(c_kvw + g * HEAD_DIM)
        wv[g, wbuf:wbuf + LANES, 0:HEAD_DIM] = new_rows(c_kvw + (N_KV + g) * HEAD_DIM)
        wk[g, :, HEAD_DIM:] = _pos_cols(wbuf + LANES, 1)
        wv[g, :, HEAD_DIM:] = _ones_col(wbuf + LANES)
    skw_ref[0:(wbuf - tq) * per_tok, :] = win_ref[tq * per_tok:wbuf * per_tok, :]
    for t in range(tq):
        for cg in range(per_tok):
            r = (wbuf - tq + t) * per_tok + cg
            skw_ref[r:r + 1, :] = zq_ref[t:t + 1, c_kvw + cg * HEAD_DIM:c_kvw + (cg + 1) * HEAD_DIM]

    return zq_ref[:, q0:q0 + D_ATT], jax.nn.sigmoid(zq_ref[:, c_gn:c_gn + LANES]), kc_ref, sk, sv, wk, wv


def _attn_sample(zq, kc, cache_win, cache_slc, page_table, *, tq):
    DB, tb, zw = zq.shape
    per_tok = 2 * N_KV
    n_pages = page_table.shape[1]
    page = cache_slc.shape[1] // per_tok
    past = n_pages * page
    wbuf = cache_win.shape[1] // per_tok
    n_seq = 2 if DB % 2 == 0 else 1
    assert wbuf + LANES == WIN_KEYS and tq <= tb and tq < CMP_STRIDE and (tq * per_tok) % SUBLANES == 0
    page_specs = [pl.BlockSpec((None, page * per_tok, HEAD_DIM),
                               functools.partial(lambda b, pt, s, j: (pt[b * n_seq + s, j], 0, 0), s=s, j=j))
                  for s in range(n_seq) for j in range(n_pages)]
    grid_spec = pltpu.PrefetchScalarGridSpec(
        num_scalar_prefetch=1,
        grid=(DB // n_seq,),
        in_specs=[
            pl.BlockSpec((n_seq, tb, zw), lambda b, pt: (b, 0, 0)),
            pl.BlockSpec((n_seq, 2 * N_KV, past // CMP_STRIDE, HEAD_DIM), lambda b, pt: (b, 0, 0, 0)),
            pl.BlockSpec((n_seq, wbuf * per_tok, HEAD_DIM), lambda b, pt: (b, 0, 0)),
        ] + page_specs,
        out_specs=[
            pl.BlockSpec((n_seq, tb, D_ATT), lambda b, pt: (b, 0, 0)),
            pl.BlockSpec((n_seq, wbuf * per_tok, HEAD_DIM), lambda b, pt: (b, 0, 0)),
        ],
        scratch_shapes=_attn_scratch(past + LANES, wbuf + LANES) * n_seq,
    )
    return pl.pallas_call(
        functools.partial(_attn_sample_kernel, n_seq=n_seq, n_pages=n_pages, page=page, tq=tq, tb=tb),
        grid_spec=grid_spec,
        out_shape=[jax.ShapeDtypeStruct((DB, tb, D_ATT), f32),
                   jax.ShapeDtypeStruct((DB, wbuf * per_tok, HEAD_DIM), f32)],
        compiler_params=_params("parallel"),
        name="attn_sample",
    )(page_table, zq, kc, cache_win, *([cache_slc] * (n_seq * n_pages)))


def _merge_kernel(x_ref, yr_ref, o_ref, gm1_ref, gm2_ref, wr_ref, wa_ref, wo_ref, out_ref):
    m = (jax.nn.sigmoid(gm1_ref[...]) * jnp.dot(yr_ref[...].astype(bf16), wr_ref[...], preferred_element_type=f32)
         + jax.nn.sigmoid(gm2_ref[...]) * jnp.dot(o_ref[...].astype(bf16), wa_ref[...], preferred_element_type=f32))
    out_ref[...] = x_ref[...] + jnp.dot(m.astype(bf16), wo_ref[...], preferred_element_type=f32)


def _merge(x2d, yr, o, z2d, wr, wa, wo):
    rows = x2d.shape[0]
    tm = min(256, rows)
    const = lambda shape: pl.BlockSpec(shape, lambda i: (0, 0), pipeline_mode=pl.Buffered(1))
    return pl.pallas_call(
        _merge_kernel,
        grid=(rows // tm,),
        in_specs=[
            pl.BlockSpec((tm, D_MODEL), lambda i: (i, 0)),
            pl.BlockSpec((tm, D_RNN), lambda i: (i, 0)),
            pl.BlockSpec((tm, D_ATT), lambda i: (i, 0)),
            pl.BlockSpec((tm, D_MODEL), lambda i: (i, COL_GM // D_MODEL)),
            pl.BlockSpec((tm, D_MODEL), lambda i: (i, COL_GM // D_MODEL + 1)),
            const((D_RNN, D_MODEL)), const((D_ATT, D_MODEL)), const((D_MODEL, D_MODEL)),
        ],
        out_specs=pl.BlockSpec((tm, D_MODEL), lambda i: (i, 0)),
        out_shape=jax.ShapeDtypeStruct((rows, D_MODEL), f32),
        compiler_params=_params("parallel"),
        name="merge",
    )(x2d, yr, o, z2d, z2d, wr, wa, wo)


def _ffn_kernel(x_ref, prev_ref, gf_ref, wg_ref, wu_ref, wd_ref, cw_ref, cb_ref, gl_ref,
                y_ref, fc_ref, h_ref, acc_ref, gcat, carry, *, S, rows, final_norm):
    ti, fi = pl.program_id(1), pl.program_id(2)
    k1 = FFN_CONV - 1
    pad = _round_up(k1 * S, SUBLANES)

    @pl.when(fi == 0)
    def _():
        h_ref[...] = _rms(x_ref[...], gf_ref[...]).astype(bf16)
        acc_ref[...] = jnp.zeros_like(acc_ref)

    @pl.when(ti == 0)
    def _():
        carry[fi, pad - k1 * S:pad, :] = prev_ref[...]

    h = h_ref[...]
    gate = jnp.dot(h, wg_ref[...], preferred_element_type=f32)
    gcat[pad - k1 * S:pad, :] = carry[fi, pad - k1 * S:pad, :]
    gcat[pad:pad + rows, :] = gate
    u = cb_ref[...] + cw_ref[k1:k1 + 1, :] * gate
    for j in range(k1):
        off = pad - (k1 - j) * S
        u = u + cw_ref[j:j + 1, :] * gcat[off:off + rows, :]
    tail = gcat[pad + rows - k1 * S:pad + rows, :]
    carry[fi, pad - k1 * S:pad, :] = tail
    fc_ref[...] = tail
    up = jnp.dot(h, wu_ref[...], preferred_element_type=f32)
    acc_ref[...] += jnp.dot((_gelu(u) * up).astype(bf16), wd_ref[...], preferred_element_type=f32)

    @pl.when(fi == pl.num_programs(2) - 1)
    def _():
        out = x_ref[...] + acc_ref[...]
        y_ref[...] = _rms(out, gl_ref[...]) if final_norm else out


def _ffn(x3, prev, g_ffn, wg, wu, wd, cw, cb, g_fin, *, S, Tt, final_norm):
    nb, rper, _ = x3.shape
    rows = Tt * S
    tf = 768
    nf = D_FF // tf
    k1 = FFN_CONV - 1
    pad = _round_up(k1 * S, SUBLANES)
    assert rper % rows == 0 and Tt >= k1
    return pl.pallas_call(
        functools.partial(_ffn_kernel, S=S, rows=rows, final_norm=final_norm),
        grid=(nb, rper // rows, nf),
        in_specs=[
            pl.BlockSpec((None, rows, D_MODEL), lambda b, t, f: (b, t, 0)),
            pl.BlockSpec((None, k1 * S, tf), lambda b, t, f: (b, 0, f)),
            pl.BlockSpec((1, D_MODEL), lambda b, t, f: (0, 0)),
            pl.BlockSpec((D_MODEL, tf), lambda b, t, f: (0, f)),
            pl.BlockSpec((D_MODEL, tf), lambda b, t, f: (0, f)),
            pl.BlockSpec((tf, D_MODEL), lambda b, t, f: (f, 0)),
            pl.BlockSpec((FFN_CONV, tf), lambda b, t, f: (0, f)),
            pl.BlockSpec((1, tf), lambda b, t, f: (0, f)),
            pl.BlockSpec((1, D_MODEL), lambda b, t, f: (0, 0)),
        ],
        out_specs=[
            pl.BlockSpec((None, rows, D_MODEL), lambda b, t, f: (b, t, 0)),
            pl.BlockSpec((None, None, k1 * S, tf), lambda b, t, f: (b, t, 0, f)),
        ],
        out_shape=[
            jax.ShapeDtypeStruct((nb, rper, D_MODEL), f32),
            jax.ShapeDtypeStruct((nb, rper // rows, k1 * S, D_FF), f32),
        ],
        scratch_shapes=[
            pltpu.VMEM((rows, D_MODEL), bf16),
            pltpu.VMEM((rows, D_MODEL), f32),
            pltpu.VMEM((pad + rows, tf), f32),
            pltpu.VMEM((nf, pad, tf), f32),
        ],
        compiler_params=_params("parallel", "arbitrary", "arbitrary"),
        name="ffn",
    )(x3, prev, g_ffn, wg, wu, wd, cw, cb, g_fin)


def _time_major(a):
    return jnp.swapaxes(a, 0, 1).reshape(1, a.shape[0] * a.shape[1], a.shape[2])


def _batch_major(a, db):
    return jnp.swapaxes(a.reshape(a.shape[1] // db, db, a.shape[2]), 0, 1)


def kernel(x_prompt, x_sample, cache_kv_cmp, cache_kv_slc, cache_kv_win, state_rnn_h, state_rnn_conv, state_ffn_conv, page_table, norm_mix, w_in, rnn_conv_w, rnn_conv_b, rnn_wa, rnn_ba, rnn_wx, rnn_bx, rnn_lambda, cmp_w, cmp_pos, w_proj_rnn, w_proj_att, w_out, norm_ffn, ffn_w_gate, ffn_w_up, ffn_conv_w, ffn_conv_b, ffn_w_down, norm_final):
    B, T, _ = x_prompt.shape
    DB, Tq, _ = x_sample.shape
    depth = w_in.shape[0]
    n_pool, page = cache_kv_cmp.shape[1:3]
    n_pages = page_table.shape[1]
    past = n_pages * page
    wbuf = cache_kv_win.shape[2]
    tb = SUBLANES
    assert page % CMP_STRIDE == 0 and page % SLC_BLOCK == 0 and Tq >= RNN_CONV - 1 and Tq <= tb
    assert Tq & (Tq - 1) == 0 and DB % SUBLANES == 0 and T % 256 == 0
    kv_shape = (2, N_KV, HEAD_DIM)

    xp = x_prompt.reshape(B * T, D_MODEL)
    xs = _time_major(x_sample)[0]
    pst = [[] for _ in range(6)]
    sst = [[] for _ in range(6)]
    g_out = norm_final.reshape(1, D_MODEL)
    for l in range(depth):
        w = w_in[l]
        widths = (D_RNN, D_RNN, D_ATT, KV_W, KV_W, KV_W, 3 * N_HEADS, 2 * D_MODEL)
        cuts = [0]
        for wd_ in widths:
            cuts.append(cuts[-1] + wd_)
        part = lambda k: w[:, cuts[k]:cuts[k + 1]]
        w_cat = jnp.concatenate(
            [part(0), part(1), part(7), part(2), part(3), part(4), part(5),
             jnp.pad(part(6), ((0, 0), (0, GN_PAD - 3 * N_HEADS)))], axis=1).astype(bf16)
        g_mix = norm_mix[l].reshape(1, D_MODEL)
        cw, cb = rnn_conv_w[l], rnn_conv_b[l].reshape(1, D_RNN)
        wgate = jnp.concatenate([rnn_wa[l], rnn_wx[l]], axis=-1).astype(bf16)
        ba, bx, lam = (v[l].reshape(1, D_RNN) for v in (rnn_ba, rnn_bx, rnn_lambda))
        cmpw = cmp_w[l].astype(bf16)
        wc = jnp.concatenate([cmpw[:, :CMP_STRIDE].reshape(2, CMP_STRIDE * HEAD_DIM, HEAD_DIM),
                              cmpw[:, CMP_STRIDE:].reshape(2, CMP_STRIDE * HEAD_DIM, HEAD_DIM)], axis=-1)
        pw = cmpw.reshape(2, CMP_BLOCK * HEAD_DIM, HEAD_DIM)
        pos = cmp_pos[l].reshape(2, 1, CMP_BLOCK * HEAD_DIM)
        wr, wa, wo = w_proj_rnn[l].astype(bf16), w_proj_att[l].astype(bf16), w_out[l].astype(bf16)
        g_ffn = norm_ffn[l].reshape(1, D_MODEL)
        fwg, fwu, fwd = ffn_w_gate[l].astype(bf16), ffn_w_up[l].astype(bf16), ffn_w_down[l].astype(bf16)
        fcw, fcb = ffn_conv_w[l], ffn_conv_b[l].reshape(1, D_FF)
        last = l == depth - 1

        per_tok = 2 * N_KV
        z, kvc_rows, kvs_rows, kvw_rows = _inproj(xp, g_mix, w_cat)
        z3 = z.reshape(B, T, D_Z)
        nbr = SUBLANES
        k1r = RNN_CONV - 1
        y_rnn, h_last, rconv = _rglru(z3, jnp.zeros((B * k1r, D_RNN), f32), jnp.zeros((B, D_RNN), f32),
                                      cw, cb, wgate, ba, bx, lam, NB=nbr, S=1, Tt=128)
        rconv = jnp.swapaxes(rconv.reshape(B // nbr, k1r, nbr, D_RNN), 1, 2).reshape(B, k1r, D_RNN)
        kc = _kc_call(kvc_rows.reshape(B, T * per_tok, HEAD_DIM), 1, B, wc, pw, pos)
        o = jnp.zeros((B, T, D_ATT), f32)
        for sb in range(T // SLC_CHUNK):
            o = _attn_prompt(z3, kc, sb, o)
        x1 = _merge(xp, y_rnn.reshape(B * T, D_RNN), o.reshape(B * T, D_ATT), z, wr, wa, wo)
        y, fconv = _ffn(x1.reshape(B, T, D_MODEL), jnp.zeros((B, FFN_CONV - 1, D_FF), f32), g_ffn, fwg, fwu, fwd,
                        fcw, fcb, g_out, S=1, Tt=512, final_norm=last)
        xp = y.reshape(B * T, D_MODEL)
        wlen = min(WINDOW, T)
        for lst, v in zip(pst, (kvc_rows.reshape(B, T, *kv_shape), kvs_rows.reshape(B, T, *kv_shape),
                                kvw_rows.reshape(B, T, *kv_shape)[:, T - wlen:], h_last, rconv, fconv[:, -1])):
            lst.append(v)

        zs = _inproj(xs, g_mix, w_cat)[0]
        zs3 = zs.reshape(1, Tq * DB, D_Z)
        ys_rnn, hs_last, sconv = _rglru(zs3, _time_major(state_rnn_conv[l])[0], state_rnn_h[l],
                                        cw, cb, wgate, ba, bx, lam, NB=1, S=DB, Tt=Tq)
        zq = _batch_major(zs3[:, :, COL_Q:], DB)
        zq = jnp.pad(zq, ((0, 0), (0, tb - Tq), (0, 0)))
        kcs = _kc_call(cache_kv_cmp[l].reshape(n_pool, page * per_tok, HEAD_DIM), n_pages, DB, wc, pw, pos,
                       page_table=page_table)
        os_, s_win = _attn_sample(zq, kcs, cache_kv_win[l].reshape(DB, wbuf * per_tok, HEAD_DIM),
                                  cache_kv_slc[l].reshape(n_pool, page * per_tok, HEAD_DIM), page_table, tq=Tq)
        os_tm = _time_major(os_[:, :Tq])[0]
        xs1 = _merge(xs, ys_rnn[0], os_tm, zs, wr, wa, wo)
        ys, sfconv = _ffn(xs1[None], _time_major(state_ffn_conv[l]), g_ffn, fwg, fwu, fwd, fcw, fcb, g_out,
                          S=DB, Tt=Tq, final_norm=last)
        xs = ys[0]
        c0 = D_ATT
        for lst, v in zip(sst, (zq[:, :Tq, c0:c0 + KV_W].reshape(DB, Tq, *kv_shape),
                                zq[:, :Tq, c0 + KV_W:c0 + 2 * KV_W].reshape(DB, Tq, *kv_shape),
                                s_win.reshape(DB, wbuf, *kv_shape), hs_last,
                                _batch_major(sconv[None], DB), _batch_major(sfconv[:, -1], DB))):
            lst.append(v)

    y_prompt = xp.reshape(B, T, D_MODEL)
    y_sample = _batch_major(xs[None], DB)
    return (y_prompt, y_sample, *[jnp.stack(v) for v in pst], *[jnp.stack(v) for v in sst])
```

```python
import functools
import math

import jax
import jax.numpy as jnp
from jax import lax
from jax.experimental import pallas as pl
from jax.experimental.pallas import tpu as pltpu

D_MODEL = 2048
D_RNN = D_MODEL // 2
RNN_BLOCKS = 8
RNN_BLK = D_RNN // RNN_BLOCKS
RNN_CONV = 4
LRU_C = 8.0
HEAD_DIM = 128
D_ATT = D_MODEL // 2
N_HEADS = D_ATT // HEAD_DIM
N_KV = 2
HPG = N_HEADS // N_KV
KV_W = 2 * N_KV * HEAD_DIM
CMP_BLOCK = 32
CMP_STRIDE = 16
SLC_BLOCK = 64
TOPK = 8
WINDOW = 512
WIN_Q_BLOCK = 128
D_FF = 3 * D_MODEL
FFN_CONV = 3
RMS_EPS = 1e-6
NEG = -1e30

COL_XR = 0
COL_GR = COL_XR + D_RNN
COL_GM = COL_GR + D_RNN
COL_Q = COL_GM + 2 * D_MODEL
COL_KVC = COL_Q + D_ATT
COL_KVS = COL_KVC + KV_W
COL_KVW = COL_KVS + KV_W
COL_GN = COL_KVW + KV_W
GN_PAD = KV_W
D_Z = COL_GN + GN_PAD

VMEM_LIMIT_BYTES = 56 * 1024 * 1024
SUBLANES = 8
LANES = 128

WIN_KEYS = WINDOW + WIN_Q_BLOCK
KC_PITCH = CMP_STRIDE * 2 * N_KV + SUBLANES
SLC_CHUNK = 512

f32 = jnp.float32
bf16 = jnp.bfloat16


def _round_up(x, m):
    return (x + m - 1) // m * m


def _params(*sem):
    return pltpu.CompilerParams(dimension_semantics=sem, vmem_limit_bytes=VMEM_LIMIT_BYTES)


def _rms(x, g):
    return (x * lax.rsqrt(jnp.mean(x * x, axis=-1, keepdims=True) + RMS_EPS)) * g


def _gelu(x):
    return x * (0.5 * (1.0 + jnp.tanh(math.sqrt(2.0 / math.pi) * (x + 0.044715 * (x * x * x)))))


def _nt(a, b):
    return lax.dot_general(a, b, (((1,), (1,)), ((), ())), preferred_element_type=f32)


def _inproj_kernel(x_ref, g_ref, w_ref, o_ref, *refs, tm, tn):
    kv_refs, h_ref = refs[:3], refs[3]

    @pl.when(pl.program_id(1) == 0)
    def _():
        h_ref[...] = _rms(x_ref[...], g_ref[...]).astype(bf16)

    o_ref[...] = jnp.dot(h_ref[...], w_ref[...], preferred_element_type=f32)

    per_tok = 2 * N_KV
    for col, kv_ref in zip((COL_KVC, COL_KVS, COL_KVW), kv_refs):
        lo = col % tn

        @pl.when(pl.program_id(1) == col // tn)
        def _(lo=lo, kv_ref=kv_ref):
            for cg in range(per_tok):
                kv_ref[pl.ds(cg, tm, stride=per_tok), :] = o_ref[:, lo + cg * HEAD_DIM:lo + (cg + 1) * HEAD_DIM]


def _inproj(x2d, g, w_cat):
    rows = x2d.shape[0]
    tm = min(1024, rows)
    tn = 1024
    per_tok = 2 * N_KV
    assert rows % tm == 0 and D_Z % tn == 0
    assert all(c % tn + KV_W <= tn for c in (COL_KVC, COL_KVS, COL_KVW))
    kv_spec = pl.BlockSpec((tm * per_tok, HEAD_DIM), lambda i, j: (i, 0))
    kv_shape = jax.ShapeDtypeStruct((rows * per_tok, HEAD_DIM), f32)
    return pl.pallas_call(
        functools.partial(_inproj_kernel, tm=tm, tn=tn),
        grid=(rows // tm, D_Z // tn),
        in_specs=[
            pl.BlockSpec((tm, D_MODEL), lambda i, j: (i, 0)),
            pl.BlockSpec((1, D_MODEL), lambda i, j: (0, 0)),
            pl.BlockSpec((D_MODEL, tn), lambda i, j: (0, j)),
        ],
        out_specs=[pl.BlockSpec((tm, tn), lambda i, j: (i, j)), kv_spec, kv_spec, kv_spec],
        out_shape=[jax.ShapeDtypeStruct((rows, D_Z), f32), kv_shape, kv_shape, kv_shape],
        scratch_shapes=[pltpu.VMEM((tm, D_MODEL), bf16)],
        compiler_params=_params("parallel", "arbitrary"),
        name="inproj",
    )(x2d, g, w_cat)


def _rglru_kernel(xr_ref, gr_ref, prev_ref, h0_ref, cw_ref, cb_ref, wg_ref, ba_ref, bx_ref, lam_ref,
                  y_ref, hl_ref, nc_ref, xt, gt, ha, hu, hc, *, NB, S, Tt):
    SE = NB * S
    rows = Tt * SE
    k1 = RNN_CONV - 1
    pad = k1 * SE
    slab = lambda n: slice(n * RNN_BLK, (n + 1) * RNN_BLK)

    @pl.when(pl.program_id(1) == 0)
    def _():
        for n in range(RNN_BLOCKS):
            xt[n, 0:pad, :] = prev_ref[:, slab(n)]
            hc[n] = h0_ref[:, slab(n)]

    def time_rows(b):
        return pl.ds(b, Tt, stride=NB) if NB > 1 else pl.ds(0, rows)

    lam = lam_ref[...]
    softplus_neg_lam = jnp.maximum(-lam, 0.0) + jnp.log1p(jnp.exp(-jnp.abs(lam)))
    for n in range(RNN_BLOCKS):
        sl = slab(n)
        for b in range(NB):
            if NB > 1:
                xt[n, pl.ds(pad + b, Tt, stride=NB), :] = xr_ref[b, :, sl]
            else:
                xt[n, pad:pad + rows, :] = xr_ref[b, :, sl]
            gt[n, time_rows(b), :] = gr_ref[b, :, sl]
        xc = cb_ref[:, sl] + cw_ref[k1:k1 + 1, sl] * xt[n, pad:pad + rows, :]
        for j in range(k1):
            off = pad - (k1 - j) * SE
            xc = xc + cw_ref[j:j + 1, sl] * xt[n, off:off + rows, :]
        tail = xt[n, rows:rows + pad, :]
        xt[n, 0:pad, :] = tail
        nc_ref[:, sl] = tail
        gates = jnp.dot(xc.astype(bf16), wg_ref[n], preferred_element_type=f32)
        r = jax.nn.sigmoid(gates[:, :RNN_BLK] + ba_ref[:, sl])
        i = jax.nn.sigmoid(gates[:, RNN_BLK:] + bx_ref[:, sl])
        log_a = (-LRU_C) * r * softplus_neg_lam[:, sl]
        a = jnp.exp(log_a)
        one_minus_a2 = -jnp.tanh(log_a) * (a * a + 1.0)
        ha[n] = a
        hu[n] = jnp.sqrt(one_minus_a2) * (i * xc)

    hs = [hc[n] for n in range(RNN_BLOCKS)]
    for t in range(Tt):
        step = slice(t * SE, (t + 1) * SE)
        for n in range(RNN_BLOCKS):
            hs[n] = ha[n, step, :] * hs[n] + hu[n, step, :]
            hu[n, step, :] = hs[n]

    for n in range(RNN_BLOCKS):
        sl = slab(n)
        hc[n] = hs[n]
        hl_ref[:, sl] = hs[n]
        gt[n] = hu[n] * _gelu(gt[n])
        for b in range(NB):
            y_ref[b, :, sl] = gt[n, time_rows(b), :]


def _rglru(z3, prev, h0, cw, cb, wg, ba, bx, lam, *, NB, S, Tt):
    groups, rper, _ = z3.shape
    SE = NB * S
    rows = Tt * SE
    k1 = RNN_CONV - 1
    pad = k1 * SE
    assert groups % NB == 0 and rper % (Tt * S) == 0 and Tt >= k1 and SE % SUBLANES == 0 and (NB == 1 or S == 1)
    vec = lambda: pl.BlockSpec((1, D_RNN), lambda b, t: (0, 0))
    slabs = lambda r: pltpu.VMEM((RNN_BLOCKS, r, RNN_BLK), f32)
    return pl.pallas_call(
        functools.partial(_rglru_kernel, NB=NB, S=S, Tt=Tt),
        grid=(groups // NB, rper // (Tt * S)),
        in_specs=[
            pl.BlockSpec((NB, Tt * S, D_RNN), lambda b, t: (b, t, COL_XR // D_RNN)),
            pl.BlockSpec((NB, Tt * S, D_RNN), lambda b, t: (b, t, COL_GR // D_RNN)),
            pl.BlockSpec((pad, D_RNN), lambda b, t: (b, 0)),
            pl.BlockSpec((SE, D_RNN), lambda b, t: (b, 0)),
            pl.BlockSpec((RNN_CONV, D_RNN), lambda b, t: (0, 0)),
            vec(),
            pl.BlockSpec((RNN_BLOCKS, RNN_BLK, 2 * RNN_BLK), lambda b, t: (0, 0, 0)),
            vec(), vec(), vec(),
        ],
        out_specs=[
            pl.BlockSpec((NB, Tt * S, D_RNN), lambda b, t: (b, t, 0)),
            pl.BlockSpec((SE, D_RNN), lambda b, t: (b, 0)),
            pl.BlockSpec((pad, D_RNN), lambda b, t: (b, 0)),
        ],
        out_shape=[
            jax.ShapeDtypeStruct((groups, rper, D_RNN), f32),
            jax.ShapeDtypeStruct((groups * S, D_RNN), f32),
            jax.ShapeDtypeStruct((groups // NB * pad, D_RNN), f32),
        ],
        scratch_shapes=[slabs(pad + rows), slabs(rows), slabs(rows), slabs(rows), slabs(SE)],
        compiler_params=_params("parallel", "arbitrary"),
        name="rglru",
    )(z3, z3, prev, h0, cw, cb, wg, ba, bx, lam)


def _kc_kernel(*refs, n_seq, n_pieces):
    refs = refs[len(refs) - (n_seq * n_pieces + 4 + n_seq):]
    x_refs = refs[:n_seq * n_pieces]
    wc_ref, pw_ref, pos_ref, kc_ref = refs[n_seq * n_pieces:n_seq * n_pieces + 4]
    xps = refs[n_seq * n_pieces + 4:]
    per_tok = 2 * N_KV
    ch = CMP_STRIDE * per_tok
    for s in range(n_seq):
        nc = 0
        for r in x_refs[s * n_pieces:(s + 1) * n_pieces]:
            for k in range(r.shape[0] // ch):
                xps[s][nc * KC_PITCH:nc * KC_PITCH + ch, :] = r[k * ch:(k + 1) * ch, :]
                nc += 1
    for c in range(2):
        groups = [jnp.concatenate([xps[s][pl.ds(l * per_tok + c * N_KV + g, nc, stride=KC_PITCH), :]
                                   for l in range(CMP_STRIDE)], axis=1) for s in range(n_seq) for g in range(N_KV)]
        xs = jnp.concatenate(groups, axis=0).astype(bf16)
        fs = jnp.dot(xs, wc_ref[c], preferred_element_type=f32)
        pos = jnp.broadcast_to(pos_ref[c], (SUBLANES, CMP_BLOCK * HEAD_DIM)).astype(bf16)
        pos_bias = jnp.dot(pos, pw_ref[c], preferred_element_type=f32)[0:1, :]
        for s in range(n_seq):
            for g in range(N_KV):
                u = s * N_KV + g
                first = fs[u * nc:(u + 1) * nc, :HEAD_DIM]
                second = fs[u * nc:(u + 1) * nc, HEAD_DIM:]
                kc_ref[s, c * N_KV + g] = first + pltpu.roll(second, nc - 1, 0) + pos_bias


def _kc_call(x, n_pieces, nseq, wc, pw, pos, page_table=None):
    assert page_table is not None or n_pieces == 1
    piece_rows = x.shape[1]
    nc = n_pieces * piece_rows // (CMP_STRIDE * 2 * N_KV)
    n_seq = 2 if nseq % 2 == 0 else 1
    nsp = 0 if page_table is None else 1
    const = lambda shape: pl.BlockSpec(shape, lambda b, *_: (0,) * len(shape))

    def piece_index(b, *pt, s, j):
        return (pt[0][b * n_seq + s, j] if pt else b * n_seq + s, 0, 0)

    grid_spec = pltpu.PrefetchScalarGridSpec(
        num_scalar_prefetch=nsp,
        grid=(nseq // n_seq,),
        in_specs=[pl.BlockSpec((None, piece_rows, HEAD_DIM), functools.partial(piece_index, s=s, j=j))
                  for s in range(n_seq) for j in range(n_pieces)] + [
            const((2, CMP_STRIDE * HEAD_DIM, 2 * HEAD_DIM)),
            const((2, CMP_BLOCK * HEAD_DIM, HEAD_DIM)),
            const((2, 1, CMP_BLOCK * HEAD_DIM)),
        ],
        out_specs=pl.BlockSpec((n_seq, 2 * N_KV, nc, HEAD_DIM), lambda b, *_: (b, 0, 0, 0)),
        scratch_shapes=[pltpu.VMEM((nc * KC_PITCH, HEAD_DIM), f32)] * n_seq,
    )
    args = ([] if page_table is None else [page_table]) + [x] * (n_seq * n_pieces) + [wc, pw, pos]
    return pl.pallas_call(
        functools.partial(_kc_kernel, n_seq=n_seq, n_pieces=n_pieces),
        grid_spec=grid_spec,
        out_shape=jax.ShapeDtypeStruct((nseq, 2 * N_KV, nc, HEAD_DIM), f32),
        compiler_params=_params("parallel"),
        name="kc_sample" if nsp else "kc_prompt",
    )(*args)


def _pos_cols(n_rows, step):
    r = lax.broadcasted_iota(jnp.int32, (n_rows, HEAD_DIM), 0) * step
    lane = lax.broadcasted_iota(jnp.int32, (n_rows, HEAD_DIM), 1)
    low_bits = SLC_BLOCK.bit_length() - 1
    hi = (r >> low_bits) << low_bits
    return jnp.where(lane == 0, hi, jnp.where(lane == 1, r - hi, 0)).astype(f32).astype(bf16)


def _keys_with_pos(k, step=1):
    return jnp.concatenate([k.astype(bf16), _pos_cols(k.shape[0], step)], axis=1)


def _ones_col(n_rows):
    lane = lax.broadcasted_iota(jnp.int32, (n_rows, HEAD_DIM), 1)
    return jnp.where(lane == 0, 1.0, 0.0).astype(bf16)


def _vals_with_ones(v):
    return jnp.concatenate([v.astype(bf16), _ones_col(v.shape[0])], axis=1)


def _exp_rows(s, mask_bias):
    s = s + mask_bias
    m = jnp.max(s, axis=-1, keepdims=True)
    e = jnp.exp(s - m)
    inv = jnp.where(m > 0.5 * NEG, 1.0 / jnp.maximum(jnp.sum(e, axis=-1, keepdims=True), 1e-30), 0.0)
    return e, inv


def _select_blocks(psum, qpos, tb):
    lanes_per_blk = SLC_BLOCK // CMP_STRIDE
    n_lane = psum.shape[1]
    imp = psum + pltpu.roll(psum, 1, 1)
    for back in range(1, lanes_per_blk):
        imp = imp + pltpu.roll(psum, n_lane - back, 1)
    lane = lax.broadcasted_iota(jnp.int32, (tb, n_lane), 1)
    lane_f = lane.astype(f32)
    blk = lane >> (lanes_per_blk.bit_length() - 1)
    cur = qpos >> (SLC_BLOCK.bit_length() - 1)
    cand = ((lane & (lanes_per_blk - 1)) == 0) & (blk >= 1) & (blk < cur)
    score = jnp.where(cand, imp, NEG)
    sel = (lane == 0) & (cur > 0)
    for _ in range(TOPK - 2):
        best = jnp.max(score, axis=-1, keepdims=True)
        first = jnp.min(jnp.where(score == best, lane_f, float(n_lane)), axis=-1, keepdims=True)
        pick = (lane_f == first) & (best > 0.5 * NEG)
        sel = sel | pick
        score = jnp.where(pick, NEG, score)
    return jnp.where(sel, 1.0, 0.0).astype(bf16)


def _attn_core(seqs, qpos, *, tb, slc_rows, win_row0, win_pos0):
    scale = 1.0 / math.sqrt(HEAD_DIM)
    units = [(seq, g) for seq in seqs for g in range(N_KV)]
    n_cmp = seqs[0][2].shape[1]
    slc_shift = SLC_BLOCK.bit_length() - 1
    cur = qpos >> slc_shift
    lane_c = lax.broadcasted_iota(jnp.int32, (tb, n_cmp), 1)
    bias_c = jnp.where(qpos - (CMP_STRIDE * lane_c + (CMP_BLOCK - 1)) >= 0, 0.0, NEG)
    lane_q = lax.broadcasted_iota(jnp.int32, (tb, HEAD_DIM), 1)
    hs = lambda x, p: x[p * tb:(p + 1) * tb]

    qgs = []
    for (q, *_), g in units:
        parts = []
        for p in range(HPG):
            h = g * HPG + p
            slope = 2.0 ** (-8.0 * (h + 1) / N_HEADS)
            parts.append(jnp.concatenate([q[:, h * HEAD_DIM:(h + 1) * HEAD_DIM] * scale,
                                          jnp.where(lane_q < 2, slope, 0.0)], axis=1))
        qgs.append(jnp.concatenate(parts, axis=0).astype(bf16))

    def scores(keys):
        return [_nt(qg, k) for qg, k in zip(qgs, keys)]

    def attend(ss, vals, biases):
        sps = [[hs(s, p) + b for p in range(HPG)] for s, b in zip(ss, biases)]
        ms = [[jnp.max(sp, axis=-1, keepdims=True) for sp in row] for row in sps]
        es = [jnp.concatenate([jnp.exp(sp - m) for sp, m in zip(row, mrow)], axis=0).astype(bf16)
              for row, mrow in zip(sps, ms)]
        pvs = [jnp.dot(e, v, preferred_element_type=f32) for e, v in zip(es, vals)]
        return [pv[:, :HEAD_DIM] / jnp.maximum(pv[:, HEAD_DIM:HEAD_DIM + 1], 1e-30) for pv in pvs]

    ss = [_nt(qg, _keys_with_pos(seq[2][g], CMP_STRIDE)) for qg, (seq, g) in zip(qgs, units)]
    p_c = []
    for s in ss:
        ps = []
        for p in range(HPG):
            e, inv = _exp_rows(hs(s, p), bias_c)
            ps.append(e * inv)
        p_c.append(ps)
    o_cmp = [jnp.dot(jnp.concatenate(ps, axis=0).astype(bf16), seq[2][N_KV + g].astype(bf16),
                     preferred_element_type=f32) for ps, (seq, g) in zip(p_c, units)]
    selm = _select_blocks(jnp.concatenate([ps[0] + ps[1] + ps[2] + ps[3] for ps in p_c], axis=0),
                          jnp.concatenate([qpos] * len(units), axis=0), len(units) * tb)

    ss_slc = scores([seq[3][g, 0:slc_rows, :] for seq, g in units])
    kpos = win_pos0 + lax.broadcasted_iota(jnp.int32, (1, WIN_KEYS), 1)
    d = qpos - kpos
    bias_w = jnp.where((d >= 0) & (d <= WINDOW) & (kpos >= 0), 0.0, NEG)
    o_win = attend(scores([seq[5][g, pl.ds(win_row0, WIN_KEYS), :] for seq, g in units]),
                   [seq[6][g, pl.ds(win_row0, WIN_KEYS), :] for seq, g in units], [bias_w] * len(units))

    kpos = lax.broadcasted_iota(jnp.int32, (1, slc_rows), 1)
    kblk = kpos >> slc_shift
    hot = lax.broadcasted_iota(jnp.int32, (n_cmp, slc_rows), 0) == kblk * (SLC_BLOCK // CMP_STRIDE)
    chosen = jnp.dot(selm, jnp.where(hot, 1.0, 0.0).astype(bf16), preferred_element_type=f32)
    local = (kblk == cur) & (qpos - kpos >= 0)
    o_slc = attend(ss_slc, [seq[4][g, 0:slc_rows, :] for seq, g in units],
                   [jnp.where((chosen[u * tb:(u + 1) * tb] > 0.5) | local, 0.0, NEG) for u in range(len(units))])

    outs = []
    for si, seq in enumerate(seqs):
        sg = seq[1]
        cols = []
        for g in range(N_KV):
            u = si * N_KV + g
            for p in range(HPG):
                h = g * HPG + p
                cols.append(sg[:, h:h + 1] * hs(o_cmp[u], p) + sg[:, N_HEADS + h:N_HEADS + h + 1] * hs(o_slc[u], p)
                            + sg[:, 2 * N_HEADS + h:2 * N_HEADS + h + 1] * hs(o_win[u], p))
        outs.append(jnp.concatenate(cols, axis=1))
    return outs


def _attn_scratch(slc_rows, win_rows):
    return [pltpu.VMEM((N_KV, rows, 2 * HEAD_DIM), bf16) for rows in (slc_rows, slc_rows, win_rows, win_rows)]


def _attn_prompt_kernel(q_ref, kvs_ref, kvw_ref, gn_ref, kc_ref, *refs, T, sb):
    o_ref, sk, sv, wk, wv = refs[-5:]
    first = len(refs) == 5
    tb = WIN_Q_BLOCK
    i = sb * (SLC_CHUNK // tb) + pl.program_id(1)

    @pl.when(pl.program_id(1) == 0)
    def _():
        for g in range(N_KV):
            for src, kdst, vdst in ((kvs_ref, sk, sv), (kvw_ref, wk, wv)):
                kdst[g] = _keys_with_pos(src[:, g * HEAD_DIM:(g + 1) * HEAD_DIM])
                vdst[g] = _vals_with_ones(src[:, (N_KV + g) * HEAD_DIM:(N_KV + g + 1) * HEAD_DIM])

    qpos = i * tb + lax.broadcasted_iota(jnp.int32, (tb, 1), 0)
    win0 = pl.multiple_of(jnp.clip(i * tb - WINDOW, 0, T - WIN_KEYS), LANES)
    res = _attn_core(
        [(q_ref[...], jax.nn.sigmoid(gn_ref[:, 0:LANES]), kc_ref, sk, sv, wk, wv)], qpos, tb=tb,
        slc_rows=(sb + 1) * SLC_CHUNK, win_row0=win0, win_pos0=win0)[0]
    if first:
        rest = o_ref.shape[0] - SLC_CHUNK

        if rest:
            @pl.when(pl.program_id(1) == 0)
            def _():
                o_ref[SLC_CHUNK:, :] = jnp.zeros((rest, D_ATT), f32)

        o_ref[pl.ds(pl.multiple_of(pl.program_id(1) * tb, tb), tb), :] = res
    else:
        o_ref[...] = res


def _attn_prompt(z3, kc, sb, o_prev):
    B, T, _ = z3.shape
    assert T % SLC_CHUNK == 0 and T >= WIN_KEYS and (o_prev is None) == (sb == 0)
    tb = WIN_Q_BLOCK
    nq = SLC_CHUNK // tb
    in_specs = [
        pl.BlockSpec((None, tb, D_ATT), lambda b, i: (b, sb * nq + i, COL_Q // D_ATT)),
        pl.BlockSpec((None, T, KV_W), lambda b, i: (b, 0, COL_KVS // KV_W)),
        pl.BlockSpec((None, T, KV_W), lambda b, i: (b, 0, COL_KVW // KV_W)),
        pl.BlockSpec((None, tb, GN_PAD), lambda b, i: (b, sb * nq + i, COL_GN // GN_PAD)),
        pl.BlockSpec((None, 2 * N_KV, T // CMP_STRIDE, HEAD_DIM), lambda b, i: (b, 0, 0, 0)),
    ]
    args = [z3, z3, z3, z3, kc]
    if o_prev is None:
        aliases = {}
        out_spec = pl.BlockSpec((None, T, D_ATT), lambda b, i: (b, 0, 0))
    else:
        aliases = {len(args): 0}
        in_specs.append(pl.BlockSpec(memory_space=pl.ANY))
        args.append(o_prev)
        out_spec = pl.BlockSpec((None, tb, D_ATT), lambda b, i: (b, sb * nq + i, 0))
    return pl.pallas_call(
        functools.partial(_attn_prompt_kernel, T=T, sb=sb),
        grid=(B, nq),
        in_specs=in_specs,
        out_specs=out_spec,
        out_shape=jax.ShapeDtypeStruct((B, T, D_ATT), f32),
        input_output_aliases=aliases,
        scratch_shapes=_attn_scratch(T, T),
        compiler_params=_params("parallel", "arbitrary"),
        name=f"attn_prompt_{sb}",
    )(*args)


def _attn_sample_kernel(*refs, n_seq, n_pages, page, tq, tb):
    zq_ref, kc_ref, win_ref = refs[1:4]
    page_refs = refs[4:4 + n_seq * n_pages]
    o_ref, skw_ref = refs[4 + n_seq * n_pages:6 + n_seq * n_pages]
    scratch = refs[6 + n_seq * n_pages:]
    past = n_pages * page
    wbuf = win_ref.shape[1] // (2 * N_KV)
    seqs = [_stage_sample_seq(zq_ref.at[s], kc_ref.at[s], win_ref.at[s], page_refs[s * n_pages:(s + 1) * n_pages],
                              skw_ref.at[s], *scratch[4 * s:4 * s + 4], page=page, tq=tq, tb=tb)
            for s in range(n_seq)]
    qpos = past + lax.broadcasted_iota(jnp.int32, (tb, 1), 0)
    outs = _attn_core(seqs, qpos, tb=tb, slc_rows=past + LANES, win_row0=0, win_pos0=past - wbuf)
    for s in range(n_seq):
        o_ref[s] = outs[s]


def _stage_sample_seq(zq_ref, kc_ref, win_ref, page_refs, skw_ref, sk, sv, wk, wv, *, page, tq, tb):
    n_pages = len(page_refs)
    per_tok = 2 * N_KV
    past = n_pages * page
    wbuf = win_ref.shape[0] // per_tok
    q0 = 0
    c_kvs = q0 + D_ATT + KV_W
    c_kvw = c_kvs + KV_W
    c_gn = c_kvw + KV_W

    def new_rows(col):
        return jnp.concatenate([zq_ref[:, col:col + HEAD_DIM], jnp.zeros((LANES - tb, HEAD_DIM), f32)],
                               axis=0).astype(bf16)

    for g in range(N_KV):
        for j in range(n_pages):
            rows = slice(j * page, (j + 1) * page)
            sk[g, rows, 0:HEAD_DIM] = page_refs[j][pl.ds(g, page, stride=per_tok), :].astype(bf16)
            sv[g, rows, 0:HEAD_DIM] = page_refs[j][pl.ds(N_KV + g, page, stride=per_tok), :].astype(bf16)
        sk[g, past:past + LANES, 0:HEAD_DIM] = new_rows(c_kvs + g * HEAD_DIM)
        sv[g, past:past + LANES, 0:HEAD_DIM] = new_rows(c_kvs + (N_KV + g) * HEAD_DIM)
        sk[g, :, HEAD_DIM:] = _pos_cols(past + LANES, 1)
        sv[g, :, HEAD_DIM:] = _ones_col(past + LANES)
        wk[g, 0:wbuf, 0:HEAD_DIM] = win_ref[pl.ds(g, wbuf, stride=per_tok), :].astype(bf16)
        wv[g, 0:wbuf, 0:HEAD_DIM] = win_ref[pl.ds(N_KV + g, wbuf, stride=per_tok), :].astype(bf16)
        wk[g, wbuf:wbuf + LANES, 0:HEAD_DIM] = new_rows(c_kvw + g * HEAD_DIM)
        wv[g, wbuf:wbuf + LANES, 0:HEAD_DIM] = new_rows(c_kvw + (N_KV + g) * HEAD_DIM)
        wk[g, :, HEAD_DIM:] = _pos_cols(wbuf + LANES, 1)
        wv[g, :, HEAD_DIM:] = _ones_col(wbuf + LANES)
    skw_ref[0:(wbuf - tq) * per_tok, :] = win_ref[tq * per_tok:wbuf * per_tok, :]
    for t in range(tq):
        for cg in range(per_tok):
            r = (wbuf - tq + t) * per_tok + cg
            skw_ref[r:r + 1, :] = zq_ref[t:t + 1, c_kvw + cg * HEAD_DIM:c_kvw + (cg + 1) * HEAD_DIM]

    return zq_ref[:, q0:q0 + D_ATT], jax.nn.sigmoid(zq_ref[:, c_gn:c_gn + LANES]), kc_ref, sk, sv, wk, wv


def _attn_sample(zq, kc, cache_win, cache_slc, page_table, *, tq):
    DB, tb, zw = zq.shape
    per_tok = 2 * N_KV
    n_pages = page_table.shape[1]
    page = cache_slc.shape[1] // per_tok
    past = n_pages * page
    wbuf = cache_win.shape[1] // per_tok
    n_seq = 2 if DB % 2 == 0 else 1
    assert wbuf + LANES == WIN_KEYS and tq <= tb and tq < CMP_STRIDE and (tq * per_tok) % SUBLANES == 0
    page_specs = [pl.BlockSpec((None, page * per_tok, HEAD_DIM),
                               functools.partial(lambda b, pt, s, j: (pt[b * n_seq + s, j], 0, 0), s=s, j=j))
                  for s in range(n_seq) for j in range(n_pages)]
    grid_spec = pltpu.PrefetchScalarGridSpec(
        num_scalar_prefetch=1,
        grid=(DB // n_seq,),
        in_specs=[
            pl.BlockSpec((n_seq, tb, zw), lambda b, pt: (b, 0, 0)),
            pl.BlockSpec((n_seq, 2 * N_KV, past // CMP_STRIDE, HEAD_DIM), lambda b, pt: (b, 0, 0, 0)),
            pl.BlockSpec((n_seq, wbuf * per_tok, HEAD_DIM), lambda b, pt: (b, 0, 0)),
        ] + page_specs,
        out_specs=[
            pl.BlockSpec((n_seq, tb, D_ATT), lambda b, pt: (b, 0, 0)),
            pl.BlockSpec((n_seq, wbuf * per_tok, HEAD_DIM), lambda b, pt: (b, 0, 0)),
        ],
        scratch_shapes=_attn_scratch(past + LANES, wbuf + LANES) * n_seq,
    )
    return pl.pallas_call(
        functools.partial(_attn_sample_kernel, n_seq=n_seq, n_pages=n_pages, page=page, tq=tq, tb=tb),
        grid_spec=grid_spec,
        out_shape=[jax.ShapeDtypeStruct((DB, tb, D_ATT), f32),
                   jax.ShapeDtypeStruct((DB, wbuf * per_tok, HEAD_DIM), f32)],
        compiler_params=_params("parallel"),
        name="attn_sample",
    )(page_table, zq, kc, cache_win, *([cache_slc] * (n_seq * n_pages)))


def _merge_kernel(x_ref, yr_ref, o_ref, gm1_ref, gm2_ref, wr_ref, wa_ref, wo_ref, out_ref):
    m = (jax.nn.sigmoid(gm1_ref[...]) * jnp.dot(yr_ref[...].astype(bf16), wr_ref[...], preferred_element_type=f32)
         + jax.nn.sigmoid(gm2_ref[...]) * jnp.dot(o_ref[...].astype(bf16), wa_ref[...], preferred_element_type=f32))
    out_ref[...] = x_ref[...] + jnp.dot(m.astype(bf16), wo_ref[...], preferred_element_type=f32)


def _merge(x2d, yr, o, z2d, wr, wa, wo):
    rows = x2d.shape[0]
    tm = min(256, rows)
    const = lambda shape: pl.BlockSpec(shape, lambda i: (0, 0), pipeline_mode=pl.Buffered(1))
    return pl.pallas_call(
        _merge_kernel,
        grid=(rows // tm,),
        in_specs=[
            pl.BlockSpec((tm, D_MODEL), lambda i: (i, 0)),
            pl.BlockSpec((tm, D_RNN), lambda i: (i, 0)),
            pl.BlockSpec((tm, D_ATT), lambda i: (i, 0)),
            pl.BlockSpec((tm, D_MODEL), lambda i: (i, COL_GM // D_MODEL)),
            pl.BlockSpec((tm, D_MODEL), lambda i: (i, COL_GM // D_MODEL + 1)),
            const((D_RNN, D_MODEL)), const((D_ATT, D_MODEL)), const((D_MODEL, D_MODEL)),
        ],
        out_specs=pl.BlockSpec((tm, D_MODEL), lambda i: (i, 0)),
        out_shape=jax.ShapeDtypeStruct((rows, D_MODEL), f32),
        compiler_params=_params("parallel"),
        name="merge",
    )(x2d, yr, o, z2d, z2d, wr, wa, wo)


def _ffn_kernel(x_ref, prev_ref, gf_ref, wg_ref, wu_ref, wd_ref, cw_ref, cb_ref, gl_ref,
                y_ref, fc_ref, h_ref, acc_ref, gcat, carry, *, S, rows, final_norm):
    ti, fi = pl.program_id(1), pl.program_id(2)
    k1 = FFN_CONV - 1
    pad = _round_up(k1 * S, SUBLANES)

    @pl.when(fi == 0)
    def _():
        h_ref[...] = _rms(x_ref[...], gf_ref[...]).astype(bf16)
        acc_ref[...] = jnp.zeros_like(acc_ref)

    @pl.when(ti == 0)
    def _():
        carry[fi, pad - k1 * S:pad, :] = prev_ref[...]

    h = h_ref[...]
    gate = jnp.dot(h, wg_ref[...], preferred_element_type=f32)
    gcat[pad - k1 * S:pad, :] = carry[fi, pad - k1 * S:pad, :]
    gcat[pad:pad + rows, :] = gate
    u = cb_ref[...] + cw_ref[k1:k1 + 1, :] * gate
    for j in range(k1):
        off = pad - (k1 - j) * S
        u = u + cw_ref[j:j + 1, :] * gcat[off:off + rows, :]
    tail = gcat[pad + rows - k1 * S:pad + rows, :]
    carry[fi, pad - k1 * S:pad, :] = tail
    fc_ref[...] = tail
    up = jnp.dot(h, wu_ref[...], preferred_element_type=f32)
    acc_ref[...] += jnp.dot((_gelu(u) * up).astype(bf16), wd_ref[...], preferred_element_type=f32)

    @pl.when(fi == pl.num_programs(2) - 1)
    def _():
        out = x_ref[...] + acc_ref[...]
        y_ref[...] = _rms(out, gl_ref[...]) if final_norm else out


def _ffn(x3, prev, g_ffn, wg, wu, wd, cw, cb, g_fin, *, S, Tt, final_norm):
    nb, rper, _ = x3.shape
    rows = Tt * S
    tf = 768
    nf = D_FF // tf
    k1 = FFN_CONV - 1
    pad = _round_up(k1 * S, SUBLANES)
    assert rper % rows == 0 and Tt >= k1
    return pl.pallas_call(
        functools.partial(_ffn_kernel, S=S, rows=rows, final_norm=final_norm),
        grid=(nb, rper // rows, nf),
        in_specs=[
            pl.BlockSpec((None, rows, D_MODEL), lambda b, t, f: (b, t, 0)),
            pl.BlockSpec((None, k1 * S, tf), lambda b, t, f: (b, 0, f)),
            pl.BlockSpec((1, D_MODEL), lambda b, t, f: (0, 0)),
            pl.BlockSpec((D_MODEL, tf), lambda b, t, f: (0, f)),
            pl.BlockSpec((D_MODEL, tf), lambda b, t, f: (0, f)),
            pl.BlockSpec((tf, D_MODEL), lambda b, t, f: (f, 0)),
            pl.BlockSpec((FFN_CONV, tf), lambda b, t, f: (0, f)),
            pl.BlockSpec((1, tf), lambda b, t, f: (0, f)),
            pl.BlockSpec((1, D_MODEL), lambda b, t, f: (0, 0)),
        ],
        out_specs=[
            pl.BlockSpec((None, rows, D_MODEL), lambda b, t, f: (b, t, 0)),
            pl.BlockSpec((None, None, k1 * S, tf), lambda b, t, f: (b, t, 0, f)),
        ],
        out_shape=[
            jax.ShapeDtypeStruct((nb, rper, D_MODEL), f32),
            jax.ShapeDtypeStruct((nb, rper // rows, k1 * S, D_FF), f32),
        ],
        scratch_shapes=[
            pltpu.VMEM((rows, D_MODEL), bf16),
            pltpu.VMEM((rows, D_MODEL), f32),
            pltpu.VMEM((pad + rows, tf), f32),
            pltpu.VMEM((nf, pad, tf), f32),
        ],
        compiler_params=_params("parallel", "arbitrary", "arbitrary"),
        name="ffn",
    )(x3, prev, g_ffn, wg, wu, wd, cw, cb, g_fin)


def _time_major(a):
    return jnp.swapaxes(a, 0, 1).reshape(1, a.shape[0] * a.shape[1], a.shape[2])


def _batch_major(a, db):
    return jnp.swapaxes(a.reshape(a.shape[1] // db, db, a.shape[2]), 0, 1)


def kernel(x_prompt, x_sample, cache_kv_cmp, cache_kv_slc, cache_kv_win, state_rnn_h, state_rnn_conv, state_ffn_conv, page_table, norm_mix, w_in, rnn_conv_w, rnn_conv_b, rnn_wa, rnn_ba, rnn_wx, rnn_bx, rnn_lambda, cmp_w, cmp_pos, w_proj_rnn, w_proj_att, w_out, norm_ffn, ffn_w_gate, ffn_w_up, ffn_conv_w, ffn_conv_b, ffn_w_down, norm_final):
    B, T, _ = x_prompt.shape
    DB, Tq, _ = x_sample.shape
    depth = w_in.shape[0]
    n_pool, page = cache_kv_cmp.shape[1:3]
    n_pages = page_table.shape[1]
    past = n_pages * page
    wbuf = cache_kv_win.shape[2]
    tb = SUBLANES
    assert page % CMP_STRIDE == 0 and page % SLC_BLOCK == 0 and Tq >= RNN_CONV - 1 and Tq <= tb
    assert Tq & (Tq - 1) == 0 and DB % SUBLANES == 0 and T % 256 == 0
    kv_shape = (2, N_KV, HEAD_DIM)

    xp = x_prompt.reshape(B * T, D_MODEL)
    xs = _time_major(x_sample)[0]
    pst = [[] for _ in range(6)]
    sst = [[] for _ in range(6)]
    g_out = norm_final.reshape(1, D_MODEL)
    for l in range(depth):
        w = w_in[l]
        widths = (D_RNN, D_RNN, D_ATT, KV_W, KV_W, KV_W, 3 * N_HEADS, 2 * D_MODEL)
        cuts = [0]
        for wd_ in widths:
            cuts.append(cuts[-1] + wd_)
        part = lambda k: w[:, cuts[k]:cuts[k + 1]]
        w_cat = jnp.concatenate(
            [part(0), part(1), part(7), part(2), part(3), part(4), part(5),
             jnp.pad(part(6), ((0, 0), (0, GN_PAD - 3 * N_HEADS)))], axis=1).astype(bf16)
        g_mix = norm_mix[l].reshape(1, D_MODEL)
        cw, cb = rnn_conv_w[l], rnn_conv_b[l].reshape(1, D_RNN)
        wgate = jnp.concatenate([rnn_wa[l], rnn_wx[l]], axis=-1).astype(bf16)
        ba, bx, lam = (v[l].reshape(1, D_RNN) for v in (rnn_ba, rnn_bx, rnn_lambda))
        cmpw = cmp_w[l].astype(bf16)
        wc = jnp.concatenate([cmpw[:, :CMP_STRIDE].reshape(2, CMP_STRIDE * HEAD_DIM, HEAD_DIM),
                              cmpw[:, CMP_STRIDE:].reshape(2, CMP_STRIDE * HEAD_DIM, HEAD_DIM)], axis=-1)
        pw = cmpw.reshape(2, CMP_BLOCK * HEAD_DIM, HEAD_DIM)
        pos = cmp_pos[l].reshape(2, 1, CMP_BLOCK * HEAD_DIM)
        wr, wa, wo = w_proj_rnn[l].astype(bf16), w_proj_att[l].astype(bf16), w_out[l].astype(bf16)
        g_ffn = norm_ffn[l].reshape(1, D_MODEL)
        fwg, fwu, fwd = ffn_w_gate[l].astype(bf16), ffn_w_up[l].astype(bf16), ffn_w_down[l].astype(bf16)
        fcw, fcb = ffn_conv_w[l], ffn_conv_b[l].reshape(1, D_FF)
        last = l == depth - 1

        per_tok = 2 * N_KV
        z, kvc_rows, kvs_rows, kvw_rows = _inproj(xp, g_mix, w_cat)
        z3 = z.reshape(B, T, D_Z)
        nbr = SUBLANES
        k1r = RNN_CONV - 1
        y_rnn, h_last, rconv = _rglru(z3, jnp.zeros((B * k1r, D_RNN), f32), jnp.zeros((B, D_RNN), f32),
                                      cw, cb, wgate, ba, bx, lam, NB=nbr, S=1, Tt=128)
        rconv = jnp.swapaxes(rconv.reshape(B // nbr, k1r, nbr, D_RNN), 1, 2).reshape(B, k1r, D_RNN)
        kc = _kc_call(kvc_rows.reshape(B, T * per_tok, HEAD_DIM), 1, B, wc, pw, pos)
        o = None
        for sb in range(T // SLC_CHUNK):
            o = _attn_prompt(z3, kc, sb, o)
        x1 = _merge(xp, y_rnn.reshape(B * T, D_RNN), o.reshape(B * T, D_ATT), z, wr, wa, wo)
        y, fconv = _ffn(x1.reshape(B, T, D_MODEL), jnp.zeros((B, FFN_CONV - 1, D_FF), f32), g_ffn, fwg, fwu, fwd,
                        fcw, fcb, g_out, S=1, Tt=512, final_norm=last)
        xp = y.reshape(B * T, D_MODEL)
        wlen = min(WINDOW, T)
        for lst, v in zip(pst, (kvc_rows.reshape(B, T, *kv_shape), kvs_rows.reshape(B, T, *kv_shape),
                                kvw_rows.reshape(B, T, *kv_shape)[:, T - wlen:], h_last, rconv, fconv[:, -1])):
            lst.append(v)

        zs = _inproj(xs, g_mix, w_cat)[0]
        zs3 = zs.reshape(1, Tq * DB, D_Z)
        ys_rnn, hs_last, sconv = _rglru(zs3, _time_major(state_rnn_conv[l])[0], state_rnn_h[l],
                                        cw, cb, wgate, ba, bx, lam, NB=1, S=DB, Tt=Tq)
        zq = _batch_major(zs3[:, :, COL_Q:], DB)
        zq = jnp.pad(zq, ((0, 0), (0, tb - Tq), (0, 0)))
        kcs = _kc_call(cache_kv_cmp[l].reshape(n_pool, page * per_tok, HEAD_DIM), n_pages, DB, wc, pw, pos,
                       page_table=page_table)
        os_, s_win = _attn_sample(zq, kcs, cache_kv_win[l].reshape(DB, wbuf * per_tok, HEAD_DIM),
                                  cache_kv_slc[l].reshape(n_pool, page * per_tok, HEAD_DIM), page_table, tq=Tq)
        os_tm = _time_major(os_[:, :Tq])[0]
        xs1 = _merge(xs, ys_rnn[0], os_tm, zs, wr, wa, wo)
        ys, sfconv = _ffn(xs1[None], _time_major(state_ffn_conv[l]), g_ffn, fwg, fwu, fwd, fcw, fcb, g_out,
                          S=DB, Tt=Tq, final_norm=last)
        xs = ys[0]
        c0 = D_ATT
        for lst, v in zip(sst, (zq[:, :Tq, c0:c0 + KV_W].reshape(DB, Tq, *kv_shape),
                                zq[:, :Tq, c0 + KV_W:c0 + 2 * KV_W].reshape(DB, Tq, *kv_shape),
                                s_win.reshape(DB, wbuf, *kv_shape), hs_last,
                                _batch_major(sconv[None], DB), _batch_major(sfconv[:, -1], DB))):
            lst.append(v)

    y_prompt = xp.reshape(B, T, D_MODEL)
    y_sample = _batch_major(xs[None], DB)
    return (y_prompt, y_sample, *[jnp.stack(v) for v in pst], *[jnp.stack(v) for v in sst])
```

```python
import functools
import math

import jax
import jax.numpy as jnp
from jax import lax
from jax.experimental import pallas as pl
from jax.experimental.pallas import tpu as pltpu

D_MODEL = 2048
D_RNN = D_MODEL // 2
RNN_BLOCKS = 8
RNN_BLK = D_RNN // RNN_BLOCKS
RNN_CONV = 4
LRU_C = 8.0
HEAD_DIM = 128
D_ATT = D_MODEL // 2
N_HEADS = D_ATT // HEAD_DIM
N_KV = 2
HPG = N_HEADS // N_KV
KV_W = 2 * N_KV * HEAD_DIM
CMP_BLOCK = 32
CMP_STRIDE = 16
SLC_BLOCK = 64
TOPK = 8
WINDOW = 512
WIN_Q_BLOCK = 128
D_FF = 3 * D_MODEL
FFN_CONV = 3
RMS_EPS = 1e-6
NEG = -1e30

COL_XR = 0
COL_GR = COL_XR + D_RNN
COL_GM = COL_GR + D_RNN
COL_Q = COL_GM + 2 * D_MODEL
COL_KVC = COL_Q + D_ATT
COL_KVS = COL_KVC + KV_W
COL_KVW = COL_KVS + KV_W
COL_GN = COL_KVW + KV_W
GN_PAD = KV_W
D_Z = COL_GN + GN_PAD

VMEM_LIMIT_BYTES = 56 * 1024 * 1024
SUBLANES = 8
LANES = 128

WIN_KEYS = WINDOW + WIN_Q_BLOCK
KC_PITCH = CMP_STRIDE * 2 * N_KV + SUBLANES
SLC_CHUNK = 512

f32 = jnp.float32
bf16 = jnp.bfloat16


def _round_up(x, m):
    return (x + m - 1) // m * m


def _params(*sem):
    return pltpu.CompilerParams(dimension_semantics=sem, vmem_limit_bytes=VMEM_LIMIT_BYTES)


def _rms(x, g):
    return (x * lax.rsqrt(jnp.mean(x * x, axis=-1, keepdims=True) + RMS_EPS)) * g


def _gelu(x):
    return x * (0.5 * (1.0 + jnp.tanh(math.sqrt(2.0 / math.pi) * (x + 0.044715 * (x * x * x)))))


def _nt(a, b):
    return lax.dot_general(a, b, (((1,), (1,)), ((), ())), preferred_element_type=f32)


def _inproj_kernel(x_ref, g_ref, w_ref, o_ref, *refs, tm, tn):
    kv_refs, h_ref = refs[:3], refs[3]

    @pl.when(pl.program_id(1) == 0)
    def _():
        h_ref[...] = _rms(x_ref[...], g_ref[...]).astype(bf16)

    o_ref[...] = jnp.dot(h_ref[...], w_ref[...], preferred_element_type=f32)

    per_tok = 2 * N_KV
    for col, kv_ref in zip((COL_KVC, COL_KVS, COL_KVW), kv_refs):
        lo = col % tn

        @pl.when(pl.program_id(1) == col // tn)
        def _(lo=lo, kv_ref=kv_ref):
            for cg in range(per_tok):
                kv_ref[pl.ds(cg, tm, stride=per_tok), :] = o_ref[:, lo + cg * HEAD_DIM:lo + (cg + 1) * HEAD_DIM]


def _inproj(x2d, g, w_cat):
    rows = x2d.shape[0]
    tm = min(1024, rows)
    tn = 1024
    per_tok = 2 * N_KV
    assert rows % tm == 0 and D_Z % tn == 0
    assert all(c % tn + KV_W <= tn for c in (COL_KVC, COL_KVS, COL_KVW))
    kv_spec = pl.BlockSpec((tm * per_tok, HEAD_DIM), lambda i, j: (i, 0))
    kv_shape = jax.ShapeDtypeStruct((rows * per_tok, HEAD_DIM), f32)
    return pl.pallas_call(
        functools.partial(_inproj_kernel, tm=tm, tn=tn),
        grid=(rows // tm, D_Z // tn),
        in_specs=[
            pl.BlockSpec((tm, D_MODEL), lambda i, j: (i, 0)),
            pl.BlockSpec((1, D_MODEL), lambda i, j: (0, 0)),
            pl.BlockSpec((D_MODEL, tn), lambda i, j: (0, j)),
        ],
        out_specs=[pl.BlockSpec((tm, tn), lambda i, j: (i, j)), kv_spec, kv_spec, kv_spec],
        out_shape=[jax.ShapeDtypeStruct((rows, D_Z), f32), kv_shape, kv_shape, kv_shape],
        scratch_shapes=[pltpu.VMEM((tm, D_MODEL), bf16)],
        compiler_params=_params("parallel", "arbitrary"),
        name="inproj",
    )(x2d, g, w_cat)


def _rglru_kernel(xr_ref, gr_ref, prev_ref, h0_ref, cw_ref, cb_ref, wg_ref, ba_ref, bx_ref, lam_ref,
                  y_ref, hl_ref, nc_ref, xt, gt, ha, hu, hc, *, NB, S, Tt):
    SE = NB * S
    rows = Tt * SE
    k1 = RNN_CONV - 1
    pad = k1 * SE
    slab = lambda n: slice(n * RNN_BLK, (n + 1) * RNN_BLK)

    @pl.when(pl.program_id(1) == 0)
    def _():
        for n in range(RNN_BLOCKS):
            xt[n, 0:pad, :] = prev_ref[:, slab(n)]
            hc[n] = h0_ref[:, slab(n)]

    def time_rows(b):
        return pl.ds(b, Tt, stride=NB) if NB > 1 else pl.ds(0, rows)

    lam = lam_ref[...]
    softplus_neg_lam = jnp.maximum(-lam, 0.0) + jnp.log1p(jnp.exp(-jnp.abs(lam)))
    for n in range(RNN_BLOCKS):
        sl = slab(n)
        for b in range(NB):
            if NB > 1:
                xt[n, pl.ds(pad + b, Tt, stride=NB), :] = xr_ref[b, :, sl]
            else:
                xt[n, pad:pad + rows, :] = xr_ref[b, :, sl]
            gt[n, time_rows(b), :] = gr_ref[b, :, sl]
        xc = cb_ref[:, sl] + cw_ref[k1:k1 + 1, sl] * xt[n, pad:pad + rows, :]
        for j in range(k1):
            off = pad - (k1 - j) * SE
            xc = xc + cw_ref[j:j + 1, sl] * xt[n, off:off + rows, :]
        tail = xt[n, rows:rows + pad, :]
        xt[n, 0:pad, :] = tail
        nc_ref[:, sl] = tail
        gates = jnp.dot(xc.astype(bf16), wg_ref[n], preferred_element_type=f32)
        r = jax.nn.sigmoid(gates[:, :RNN_BLK] + ba_ref[:, sl])
        i = jax.nn.sigmoid(gates[:, RNN_BLK:] + bx_ref[:, sl])
        log_a = (-LRU_C) * r * softplus_neg_lam[:, sl]
        a = jnp.exp(log_a)
        one_minus_a2 = -jnp.tanh(log_a) * (a * a + 1.0)
        ha[n] = a
        hu[n] = jnp.sqrt(one_minus_a2) * (i * xc)

    hs = [hc[n] for n in range(RNN_BLOCKS)]
    for t in range(Tt):
        step = slice(t * SE, (t + 1) * SE)
        for n in range(RNN_BLOCKS):
            hs[n] = ha[n, step, :] * hs[n] + hu[n, step, :]
            hu[n, step, :] = hs[n]

    for n in range(RNN_BLOCKS):
        sl = slab(n)
        hc[n] = hs[n]
        hl_ref[:, sl] = hs[n]
        gt[n] = hu[n] * _gelu(gt[n])
        for b in range(NB):
            y_ref[b, :, sl] = gt[n, time_rows(b), :]


def _rglru(z3, prev, h0, cw, cb, wg, ba, bx, lam, *, NB, S, Tt):
    groups, rper, _ = z3.shape
    SE = NB * S
    rows = Tt * SE
    k1 = RNN_CONV - 1
    pad = k1 * SE
    assert groups % NB == 0 and rper % (Tt * S) == 0 and Tt >= k1 and SE % SUBLANES == 0 and (NB == 1 or S == 1)
    vec = lambda: pl.BlockSpec((1, D_RNN), lambda b, t: (0, 0))
    slabs = lambda r: pltpu.VMEM((RNN_BLOCKS, r, RNN_BLK), f32)
    return pl.pallas_call(
        functools.partial(_rglru_kernel, NB=NB, S=S, Tt=Tt),
        grid=(groups // NB, rper // (Tt * S)),
        in_specs=[
            pl.BlockSpec((NB, Tt * S, D_RNN), lambda b, t: (b, t, COL_XR // D_RNN)),
            pl.BlockSpec((NB, Tt * S, D_RNN), lambda b, t: (b, t, COL_GR // D_RNN)),
            pl.BlockSpec((pad, D_RNN), lambda b, t: (b, 0)),
            pl.BlockSpec((SE, D_RNN), lambda b, t: (b, 0)),
            pl.BlockSpec((RNN_CONV, D_RNN), lambda b, t: (0, 0)),
            vec(),
            pl.BlockSpec((RNN_BLOCKS, RNN_BLK, 2 * RNN_BLK), lambda b, t: (0, 0, 0)),
            vec(), vec(), vec(),
        ],
        out_specs=[
            pl.BlockSpec((NB, Tt * S, D_RNN), lambda b, t: (b, t, 0)),
            pl.BlockSpec((SE, D_RNN), lambda b, t: (b, 0)),
            pl.BlockSpec((pad, D_RNN), lambda b, t: (b, 0)),
        ],
        out_shape=[
            jax.ShapeDtypeStruct((groups, rper, D_RNN), f32),
            jax.ShapeDtypeStruct((groups * S, D_RNN), f32),
            jax.ShapeDtypeStruct((groups // NB * pad, D_RNN), f32),
        ],
        scratch_shapes=[slabs(pad + rows), slabs(rows), slabs(rows), slabs(rows), slabs(SE)],
        compiler_params=_params("parallel", "arbitrary"),
        name="rglru",
    )(z3, z3, prev, h0, cw, cb, wg, ba, bx, lam)


def _kc_kernel(*refs, n_seq, n_pieces):
    refs = refs[len(refs) - (n_seq * n_pieces + 4 + n_seq):]
    x_refs = refs[:n_seq * n_pieces]
    wc_ref, pw_ref, pos_ref, kc_ref = refs[n_seq * n_pieces:n_seq * n_pieces + 4]
    xps = refs[n_seq * n_pieces + 4:]
    per_tok = 2 * N_KV
    ch = CMP_STRIDE * per_tok
    for s in range(n_seq):
        nc = 0
        for r in x_refs[s * n_pieces:(s + 1) * n_pieces]:
            for k in range(r.shape[0] // ch):
                xps[s][nc * KC_PITCH:nc * KC_PITCH + ch, :] = r[k * ch:(k + 1) * ch, :]
                nc += 1
    for c in range(2):
        groups = [jnp.concatenate([xps[s][pl.ds(l * per_tok + c * N_KV + g, nc, stride=KC_PITCH), :]
                                   for l in range(CMP_STRIDE)], axis=1) for s in range(n_seq) for g in range(N_KV)]
        xs = jnp.concatenate(groups, axis=0).astype(bf16)
        fs = jnp.dot(xs, wc_ref[c], preferred_element_type=f32)
        pos = jnp.broadcast_to(pos_ref[c], (SUBLANES, CMP_BLOCK * HEAD_DIM)).astype(bf16)
        pos_bias = jnp.dot(pos, pw_ref[c], preferred_element_type=f32)[0:1, :]
        for s in range(n_seq):
            for g in range(N_KV):
                u = s * N_KV + g
                first = fs[u * nc:(u + 1) * nc, :HEAD_DIM]
                second = fs[u * nc:(u + 1) * nc, HEAD_DIM:]
                kc_ref[s, c * N_KV + g] = first + pltpu.roll(second, nc - 1, 0) + pos_bias


def _kc_call(x, n_pieces, nseq, wc, pw, pos, page_table=None):
    assert page_table is not None or n_pieces == 1
    piece_rows = x.shape[1]
    nc = n_pieces * piece_rows // (CMP_STRIDE * 2 * N_KV)
    n_seq = 2 if nseq % 2 == 0 else 1
    nsp = 0 if page_table is None else 1
    const = lambda shape: pl.BlockSpec(shape, lambda b, *_: (0,) * len(shape))

    def piece_index(b, *pt, s, j):
        return (pt[0][b * n_seq + s, j] if pt else b * n_seq + s, 0, 0)

    grid_spec = pltpu.PrefetchScalarGridSpec(
        num_scalar_prefetch=nsp,
        grid=(nseq // n_seq,),
        in_specs=[pl.BlockSpec((None, piece_rows, HEAD_DIM), functools.partial(piece_index, s=s, j=j))
                  for s in range(n_seq) for j in range(n_pieces)] + [
            const((2, CMP_STRIDE * HEAD_DIM, 2 * HEAD_DIM)),
            const((2, CMP_BLOCK * HEAD_DIM, HEAD_DIM)),
            const((2, 1, CMP_BLOCK * HEAD_DIM)),
        ],
        out_specs=pl.BlockSpec((n_seq, 2 * N_KV, nc, HEAD_DIM), lambda b, *_: (b, 0, 0, 0)),
        scratch_shapes=[pltpu.VMEM((nc * KC_PITCH, HEAD_DIM), f32)] * n_seq,
    )
    args = ([] if page_table is None else [page_table]) + [x] * (n_seq * n_pieces) + [wc, pw, pos]
    return pl.pallas_call(
        functools.partial(_kc_kernel, n_seq=n_seq, n_pieces=n_pieces),
        grid_spec=grid_spec,
        out_shape=jax.ShapeDtypeStruct((nseq, 2 * N_KV, nc, HEAD_DIM), f32),
        compiler_params=_params("parallel"),
        name="kc_sample" if nsp else "kc_prompt",
    )(*args)


def _pos_cols(n_rows, step):
    r = lax.broadcasted_iota(jnp.int32, (n_rows, HEAD_DIM), 0) * step
    lane = lax.broadcasted_iota(jnp.int32, (n_rows, HEAD_DIM), 1)
    low_bits = SLC_BLOCK.bit_length() - 1
    hi = (r >> low_bits) << low_bits
    return jnp.where(lane == 0, hi, jnp.where(lane == 1, r - hi, 0)).astype(f32).astype(bf16)


def _keys_with_pos(k, step=1):
    return jnp.concatenate([k.astype(bf16), _pos_cols(k.shape[0], step)], axis=1)


def _ones_col(n_rows):
    lane = lax.broadcasted_iota(jnp.int32, (n_rows, HEAD_DIM), 1)
    return jnp.where(lane == 0, 1.0, 0.0).astype(bf16)


def _vals_with_ones(v):
    return jnp.concatenate([v.astype(bf16), _ones_col(v.shape[0])], axis=1)


def _exp_rows(s, mask_bias):
    s = s + mask_bias
    m = jnp.max(s, axis=-1, keepdims=True)
    e = jnp.exp(s - m)
    inv = jnp.where(m > 0.5 * NEG, 1.0 / jnp.maximum(jnp.sum(e, axis=-1, keepdims=True), 1e-30), 0.0)
    return e, inv


def _select_blocks(psum, qpos, tb):
    lanes_per_blk = SLC_BLOCK // CMP_STRIDE
    n_lane = psum.shape[1]
    imp = psum + pltpu.roll(psum, 1, 1)
    for back in range(1, lanes_per_blk):
        imp = imp + pltpu.roll(psum, n_lane - back, 1)
    lane = lax.broadcasted_iota(jnp.int32, (tb, n_lane), 1)
    lane_f = lane.astype(f32)
    blk = lane >> (lanes_per_blk.bit_length() - 1)
    cur = qpos >> (SLC_BLOCK.bit_length() - 1)
    cand = ((lane & (lanes_per_blk - 1)) == 0) & (blk >= 1) & (blk < cur)
    score = jnp.where(cand, imp, NEG)
    sel = (lane == 0) & (cur > 0)
    for _ in range(TOPK - 2):
        best = jnp.max(score, axis=-1, keepdims=True)
        first = jnp.min(jnp.where(score == best, lane_f, float(n_lane)), axis=-1, keepdims=True)
        pick = (lane_f == first) & (best > 0.5 * NEG)
        sel = sel | pick
        score = jnp.where(pick, NEG, score)
    return jnp.where(sel, 1.0, 0.0).astype(bf16)


def _attn_core(seqs, *, tb, slc_rows):
    scale = 1.0 / math.sqrt(HEAD_DIM)
    units = [(seq, g) for seq in seqs for g in range(N_KV)]
    n_cmp = seqs[0][2].shape[1]
    slc_shift = SLC_BLOCK.bit_length() - 1
    lane_c = lax.broadcasted_iota(jnp.int32, (tb, n_cmp), 1)
    bias_cs = [jnp.where(seq[7] - (CMP_STRIDE * lane_c + (CMP_BLOCK - 1)) >= 0, 0.0, NEG) for seq in seqs]
    lane_q = lax.broadcasted_iota(jnp.int32, (tb, HEAD_DIM), 1)
    hs = lambda x, p: x[p * tb:(p + 1) * tb]

    qgs = []
    for (q, *_), g in units:
        parts = []
        for p in range(HPG):
            h = g * HPG + p
            slope = 2.0 ** (-8.0 * (h + 1) / N_HEADS)
            parts.append(jnp.concatenate([q[:, h * HEAD_DIM:(h + 1) * HEAD_DIM] * scale,
                                          jnp.where(lane_q < 2, slope, 0.0)], axis=1))
        qgs.append(jnp.concatenate(parts, axis=0).astype(bf16))

    def scores(keys):
        return [_nt(qg, k) for qg, k in zip(qgs, keys)]

    def attend(ss, vals, biases):
        sps = [[hs(s, p) + b for p in range(HPG)] for s, b in zip(ss, biases)]
        ms = [[jnp.max(sp, axis=-1, keepdims=True) for sp in row] for row in sps]
        es = [jnp.concatenate([jnp.exp(sp - m) for sp, m in zip(row, mrow)], axis=0).astype(bf16)
              for row, mrow in zip(sps, ms)]
        pvs = [jnp.dot(e, v, preferred_element_type=f32) for e, v in zip(es, vals)]
        return [pv[:, :HEAD_DIM] / jnp.maximum(pv[:, HEAD_DIM:HEAD_DIM + 1], 1e-30) for pv in pvs]

    ss = [_nt(qg, _keys_with_pos(seq[2][g], CMP_STRIDE)) for qg, (seq, g) in zip(qgs, units)]
    p_c = []
    for u, s in enumerate(ss):
        ps = []
        for p in range(HPG):
            e, inv = _exp_rows(hs(s, p), bias_cs[u // N_KV])
            ps.append(e * inv)
        p_c.append(ps)
    o_cmp = [jnp.dot(jnp.concatenate(ps, axis=0).astype(bf16), seq[2][N_KV + g].astype(bf16),
                     preferred_element_type=f32) for ps, (seq, g) in zip(p_c, units)]
    selm = _select_blocks(jnp.concatenate([ps[0] + ps[1] + ps[2] + ps[3] for ps in p_c], axis=0),
                          jnp.concatenate([seq[7] for seq, _ in units], axis=0), len(units) * tb)

    ss_slc = scores([seq[3][g, 0:slc_rows, :] for seq, g in units])
    bias_ws = []
    for seq in seqs:
        kpos = seq[9] + lax.broadcasted_iota(jnp.int32, (1, WIN_KEYS), 1)
        d = seq[7] - kpos
        bias_ws.append(jnp.where((d >= 0) & (d <= WINDOW) & (kpos >= 0), 0.0, NEG))
    o_win = attend(scores([seq[5][g, pl.ds(seq[8], WIN_KEYS), :] for seq, g in units]),
                   [seq[6][g, pl.ds(seq[8], WIN_KEYS), :] for seq, g in units],
                   [bias_ws[u // N_KV] for u in range(len(units))])

    kpos = lax.broadcasted_iota(jnp.int32, (1, slc_rows), 1)
    kblk = kpos >> slc_shift
    hot = lax.broadcasted_iota(jnp.int32, (n_cmp, slc_rows), 0) == kblk * (SLC_BLOCK // CMP_STRIDE)
    chosen = jnp.dot(selm, jnp.where(hot, 1.0, 0.0).astype(bf16), preferred_element_type=f32)
    local = [(kblk == (seq[7] >> slc_shift)) & (seq[7] - kpos >= 0) for seq in seqs]
    o_slc = attend(ss_slc, [seq[4][g, 0:slc_rows, :] for seq, g in units],
                   [jnp.where((chosen[u * tb:(u + 1) * tb] > 0.5) | local[u // N_KV], 0.0, NEG)
                    for u in range(len(units))])

    outs = []
    for si, seq in enumerate(seqs):
        sg = seq[1]
        cols = []
        for g in range(N_KV):
            u = si * N_KV + g
            for p in range(HPG):
                h = g * HPG + p
                cols.append(sg[:, h:h + 1] * hs(o_cmp[u], p) + sg[:, N_HEADS + h:N_HEADS + h + 1] * hs(o_slc[u], p)
                            + sg[:, 2 * N_HEADS + h:2 * N_HEADS + h + 1] * hs(o_win[u], p))
        outs.append(jnp.concatenate(cols, axis=1))
    return outs


def _attn_scratch(slc_rows, win_rows):
    return [pltpu.VMEM((N_KV, rows, 2 * HEAD_DIM), bf16) for rows in (slc_rows, slc_rows, win_rows, win_rows)]


def _attn_prompt_kernel(q_ref, kvs_ref, kvw_ref, gn_ref, kc_ref, *refs, T, sb):
    o_ref, sk, sv, wk, wv = refs[-5:]
    first = len(refs) == 5
    tb = WIN_Q_BLOCK
    nqb = q_ref.shape[0] // tb
    i0 = (sb * (SLC_CHUNK // tb)) + pl.program_id(1) * nqb

    @pl.when(pl.program_id(1) == 0)
    def _():
        for g in range(N_KV):
            for src, kdst, vdst in ((kvs_ref, sk, sv), (kvw_ref, wk, wv)):
                kdst[g] = _keys_with_pos(src[:, g * HEAD_DIM:(g + 1) * HEAD_DIM])
                vdst[g] = _vals_with_ones(src[:, (N_KV + g) * HEAD_DIM:(N_KV + g + 1) * HEAD_DIM])

    seqs = []
    for k in range(nqb):
        rows = slice(k * tb, (k + 1) * tb)
        qpos = (i0 + k) * tb + lax.broadcasted_iota(jnp.int32, (tb, 1), 0)
        win0 = pl.multiple_of(jnp.clip((i0 + k) * tb - WINDOW, 0, T - WIN_KEYS), LANES)
        seqs.append((q_ref[rows, :], jax.nn.sigmoid(gn_ref[rows, 0:LANES]), kc_ref, sk, sv, wk, wv, qpos, win0, win0))
    res = jnp.concatenate(_attn_core(seqs, tb=tb, slc_rows=(sb + 1) * SLC_CHUNK), axis=0)
    if first:
        rest = o_ref.shape[0] - SLC_CHUNK

        if rest:
            @pl.when(pl.program_id(1) == 0)
            def _():
                o_ref[SLC_CHUNK:, :] = jnp.zeros((rest, D_ATT), f32)

        o_ref[pl.ds(pl.multiple_of(pl.program_id(1) * nqb * tb, nqb * tb), nqb * tb), :] = res
    else:
        o_ref[...] = res


def _attn_prompt(z3, kc, sb, o_prev):
    B, T, _ = z3.shape
    assert T % SLC_CHUNK == 0 and T >= WIN_KEYS and (o_prev is None) == (sb == 0)
    tb = WIN_Q_BLOCK * (2 if sb >= 2 else 1)
    nq = SLC_CHUNK // tb
    in_specs = [
        pl.BlockSpec((None, tb, D_ATT), lambda b, i: (b, sb * nq + i, COL_Q // D_ATT)),
        pl.BlockSpec((None, T, KV_W), lambda b, i: (b, 0, COL_KVS // KV_W)),
        pl.BlockSpec((None, T, KV_W), lambda b, i: (b, 0, COL_KVW // KV_W)),
        pl.BlockSpec((None, tb, GN_PAD), lambda b, i: (b, sb * nq + i, COL_GN // GN_PAD)),
        pl.BlockSpec((None, 2 * N_KV, T // CMP_STRIDE, HEAD_DIM), lambda b, i: (b, 0, 0, 0)),
    ]
    args = [z3, z3, z3, z3, kc]
    if o_prev is None:
        aliases = {}
        out_spec = pl.BlockSpec((None, T, D_ATT), lambda b, i: (b, 0, 0))
    else:
        aliases = {len(args): 0}
        in_specs.append(pl.BlockSpec(memory_space=pl.ANY))
        args.append(o_prev)
        out_spec = pl.BlockSpec((None, tb, D_ATT), lambda b, i: (b, sb * nq + i, 0))
    return pl.pallas_call(
        functools.partial(_attn_prompt_kernel, T=T, sb=sb),
        grid=(B, nq),
        in_specs=in_specs,
        out_specs=out_spec,
        out_shape=jax.ShapeDtypeStruct((B, T, D_ATT), f32),
        input_output_aliases=aliases,
        scratch_shapes=_attn_scratch(T, T),
        compiler_params=_params("parallel", "arbitrary"),
        name=f"attn_prompt_{sb}",
    )(*args)


def _attn_sample_kernel(*refs, n_seq, n_pages, page, tq, tb):
    zq_ref, kc_ref, win_ref = refs[1:4]
    page_refs = refs[4:4 + n_seq * n_pages]
    o_ref, skw_ref = refs[4 + n_seq * n_pages:6 + n_seq * n_pages]
    scratch = refs[6 + n_seq * n_pages:]
    past = n_pages * page
    wbuf = win_ref.shape[1] // (2 * N_KV)
    seqs = [_stage_sample_seq(zq_ref.at[s], kc_ref.at[s], win_ref.at[s], page_refs[s * n_pages:(s + 1) * n_pages],
                              skw_ref.at[s], *scratch[4 * s:4 * s + 4], page=page, tq=tq, tb=tb)
            for s in range(n_seq)]
    qpos = past + lax.broadcasted_iota(jnp.int32, (tb, 1), 0)
    outs = _attn_core([seq + (qpos, 0, past - wbuf) for seq in seqs], tb=tb, slc_rows=past + LANES)
    for s in range(n_seq):
        o_ref[s] = outs[s]


def _stage_sample_seq(zq_ref, kc_ref, win_ref, page_refs, skw_ref, sk, sv, wk, wv, *, page, tq, tb):
    n_pages = len(page_refs)
    per_tok = 2 * N_KV
    past = n_pages * page
    wbuf = win_ref.shape[0] // per_tok
    q0 = 0
    c_kvs = q0 + D_ATT + KV_W
    c_kvw = c_kvs + KV_W
    c_gn = c_kvw + KV_W

    def new_rows(col):
        return jnp.concatenate([zq_ref[:, col:col + HEAD_DIM], jnp.zeros((LANES - tb, HEAD_DIM), f32)],
                               axis=0).astype(bf16)

    for g in range(N_KV):
        for j in range(n_pages):
            rows = slice(j * page, (j + 1) * page)
            sk[g, rows, 0:HEAD_DIM] = page_refs[j][pl.ds(g, page, stride=per_tok), :].astype(bf16)
            sv[g, rows, 0:HEAD_DIM] = page_refs[j][pl.ds(N_KV + g, page, stride=per_tok), :].astype(bf16)
        sk[g, past:past + LANES, 0:HEAD_DIM] = new_rows(c_kvs + g * HEAD_DIM)
        sv[g, past:past + LANES, 0:HEAD_DIM] = new_rows(c_kvs + (N_KV + g) * HEAD_DIM)
        sk[g, :, HEAD_DIM:] = _pos_cols(past + LANES, 1)
        sv[g, :, HEAD_DIM:] = _ones_col(past + LANES)
        wk[g, 0:wbuf, 0:HEAD_DIM] = win_ref[pl.ds(g, wbuf, stride=per_tok), :].astype(bf16)
        wv[g, 0:wbuf, 0:HEAD_DIM] = win_ref[pl.ds(N_KV + g, wbuf, stride=per_tok), :].astype(bf16)
        wk[g, wbuf:wbuf + LANES, 0:HEAD_DIM] = new_rows(c_kvw + g * HEAD_DIM)
        wv[g, wbuf:wbuf + LANES, 0:HEAD_DIM] = new_rows(c_kvw + (N_KV + g) * HEAD_DIM)
        wk[g, :, HEAD_DIM:] = _pos_cols(wbuf + LANES, 1)
        wv[g, :, HEAD_DIM:] = _ones_col(wbuf + LANES)
    skw_ref[0:(wbuf - tq) * per_tok, :] = win_ref[tq * per_tok:wbuf * per_tok, :]
    for t in range(tq):
        for cg in range(per_tok):
            r = (wbuf - tq + t) * per_tok + cg
            skw_ref[r:r + 1, :] = zq_ref[t:t + 1, c_kvw + cg * HEAD_DIM:c_kvw + (cg + 1) * HEAD_DIM]

    return zq_ref[:, q0:q0 + D_ATT], jax.nn.sigmoid(zq_ref[:, c_gn:c_gn + LANES]), kc_ref, sk, sv, wk, wv


def _attn_sample(zq, kc, cache_win, cache_slc, page_table, *, tq):
    DB, tb, zw = zq.shape
    per_tok = 2 * N_KV
    n_pages = page_table.shape[1]
    page = cache_slc.shape[1] // per_tok
    past = n_pages * page
    wbuf = cache_win.shape[1] // per_tok
    n_seq = 2 if DB % 2 == 0 else 1
    assert wbuf + LANES == WIN_KEYS and tq <= tb and tq < CMP_STRIDE and (tq * per_tok) % SUBLANES == 0
    page_specs = [pl.BlockSpec((None, page * per_tok, HEAD_DIM),
                               functools.partial(lambda b, pt, s, j: (pt[b * n_seq + s, j], 0, 0), s=s, j=j))
                  for s in range(n_seq) for j in range(n_pages)]
    grid_spec = pltpu.PrefetchScalarGridSpec(
        num_scalar_prefetch=1,
        grid=(DB // n_seq,),
        in_specs=[
            pl.BlockSpec((n_seq, tb, zw), lambda b, pt: (b, 0, 0)),
            pl.BlockSpec((n_seq, 2 * N_KV, past // CMP_STRIDE, HEAD_DIM), lambda b, pt: (b, 0, 0, 0)),
            pl.BlockSpec((n_seq, wbuf * per_tok, HEAD_DIM), lambda b, pt: (b, 0, 0)),
        ] + page_specs,
        out_specs=[
            pl.BlockSpec((n_seq, tb, D_ATT), lambda b, pt: (b, 0, 0)),
            pl.BlockSpec((n_seq, wbuf * per_tok, HEAD_DIM), lambda b, pt: (b, 0, 0)),
        ],
        scratch_shapes=_attn_scratch(past + LANES, wbuf + LANES) * n_seq,
    )
    return pl.pallas_call(
        functools.partial(_attn_sample_kernel, n_seq=n_seq, n_pages=n_pages, page=page, tq=tq, tb=tb),
        grid_spec=grid_spec,
        out_shape=[jax.ShapeDtypeStruct((DB, tb, D_ATT), f32),
                   jax.ShapeDtypeStruct((DB, wbuf * per_tok, HEAD_DIM), f32)],
        compiler_params=_params("parallel"),
        name="attn_sample",
    )(page_table, zq, kc, cache_win, *([cache_slc] * (n_seq * n_pages)))


def _merge_kernel(x_ref, yr_ref, o_ref, gm1_ref, gm2_ref, wr_ref, wa_ref, wo_ref, out_ref):
    m = (jax.nn.sigmoid(gm1_ref[...]) * jnp.dot(yr_ref[...].astype(bf16), wr_ref[...], preferred_element_type=f32)
         + jax.nn.sigmoid(gm2_ref[...]) * jnp.dot(o_ref[...].astype(bf16), wa_ref[...], preferred_element_type=f32))
    out_ref[...] = x_ref[...] + jnp.dot(m.astype(bf16), wo_ref[...], preferred_element_type=f32)


def _merge(x2d, yr, o, z2d, wr, wa, wo):
    rows = x2d.shape[0]
    tm = min(256, rows)
    const = lambda shape: pl.BlockSpec(shape, lambda i: (0, 0), pipeline_mode=pl.Buffered(1))
    return pl.pallas_call(
        _merge_kernel,
        grid=(rows // tm,),
        in_specs=[
            pl.BlockSpec((tm, D_MODEL), lambda i: (i, 0)),
            pl.BlockSpec((tm, D_RNN), lambda i: (i, 0)),
            pl.BlockSpec((tm, D_ATT), lambda i: (i, 0)),
            pl.BlockSpec((tm, D_MODEL), lambda i: (i, COL_GM // D_MODEL)),
            pl.BlockSpec((tm, D_MODEL), lambda i: (i, COL_GM // D_MODEL + 1)),
            const((D_RNN, D_MODEL)), const((D_ATT, D_MODEL)), const((D_MODEL, D_MODEL)),
        ],
        out_specs=pl.BlockSpec((tm, D_MODEL), lambda i: (i, 0)),
        out_shape=jax.ShapeDtypeStruct((rows, D_MODEL), f32),
        compiler_params=_params("parallel"),
        name="merge",
    )(x2d, yr, o, z2d, z2d, wr, wa, wo)


def _ffn_kernel(x_ref, prev_ref, gf_ref, wg_ref, wu_ref, wd_ref, cw_ref, cb_ref, gl_ref,
                y_ref, fc_ref, h_ref, gcat, carry, *, S, rows, final_norm):
    ti, fi = pl.program_id(1), pl.program_id(2)
    k1 = FFN_CONV - 1
    pad = _round_up(k1 * S, SUBLANES)

    @pl.when(fi == 0)
    def _():
        h_ref[...] = _rms(x_ref[...], gf_ref[...]).astype(bf16)
        y_ref[...] = jnp.zeros_like(y_ref)

    @pl.when(ti == 0)
    def _():
        carry[fi, pad - k1 * S:pad, :] = prev_ref[...]

    h = h_ref[...]
    gate = jnp.dot(h, wg_ref[...], preferred_element_type=f32)
    gcat[pad - k1 * S:pad, :] = carry[fi, pad - k1 * S:pad, :]
    gcat[pad:pad + rows, :] = gate
    u = cb_ref[...] + cw_ref[k1:k1 + 1, :] * gate
    for j in range(k1):
        off = pad - (k1 - j) * S
        u = u + cw_ref[j:j + 1, :] * gcat[off:off + rows, :]
    tail = gcat[pad + rows - k1 * S:pad + rows, :]
    carry[fi, pad - k1 * S:pad, :] = tail
    fc_ref[...] = tail
    up = jnp.dot(h, wu_ref[...], preferred_element_type=f32)
    y_ref[...] += jnp.dot((_gelu(u) * up).astype(bf16), wd_ref[...], preferred_element_type=f32)

    @pl.when(fi == pl.num_programs(2) - 1)
    def _():
        out = x_ref[...] + y_ref[...]
        y_ref[...] = _rms(out, gl_ref[...]) if final_norm else out


def _ffn(x3, prev, g_ffn, wg, wu, wd, cw, cb, g_fin, *, S, Tt, final_norm):
    nb, rper, _ = x3.shape
    rows = Tt * S
    tf = 1024
    nf = D_FF // tf
    k1 = FFN_CONV - 1
    pad = _round_up(k1 * S, SUBLANES)
    assert rper % rows == 0 and Tt >= k1
    return pl.pallas_call(
        functools.partial(_ffn_kernel, S=S, rows=rows, final_norm=final_norm),
        grid=(nb, rper // rows, nf),
        in_specs=[
            pl.BlockSpec((None, rows, D_MODEL), lambda b, t, f: (b, t, 0)),
            pl.BlockSpec((None, k1 * S, tf), lambda b, t, f: (b, 0, f)),
            pl.BlockSpec((1, D_MODEL), lambda b, t, f: (0, 0)),
            pl.BlockSpec((D_MODEL, tf), lambda b, t, f: (0, f)),
            pl.BlockSpec((D_MODEL, tf), lambda b, t, f: (0, f)),
            pl.BlockSpec((tf, D_MODEL), lambda b, t, f: (f, 0)),
            pl.BlockSpec((FFN_CONV, tf), lambda b, t, f: (0, f)),
            pl.BlockSpec((1, tf), lambda b, t, f: (0, f)),
            pl.BlockSpec((1, D_MODEL), lambda b, t, f: (0, 0)),
        ],
        out_specs=[
            pl.BlockSpec((None, rows, D_MODEL), lambda b, t, f: (b, t, 0)),
            pl.BlockSpec((None, None, k1 * S, tf), lambda b, t, f: (b, t, 0, f)),
        ],
        out_shape=[
            jax.ShapeDtypeStruct((nb, rper, D_MODEL), f32),
            jax.ShapeDtypeStruct((nb, rper // rows, k1 * S, D_FF), f32),
        ],
        scratch_shapes=[
            pltpu.VMEM((rows, D_MODEL), bf16),
            pltpu.VMEM((pad + rows, tf), f32),
            pltpu.VMEM((nf, pad, tf), f32),
        ],
        compiler_params=_params("parallel", "arbitrary", "arbitrary"),
        name="ffn",
    )(x3, prev, g_ffn, wg, wu, wd, cw, cb, g_fin)


def _time_major(a):
    return jnp.swapaxes(a, 0, 1).reshape(1, a.shape[0] * a.shape[1], a.shape[2])


def _batch_major(a, db):
    return jnp.swapaxes(a.reshape(a.shape[1] // db, db, a.shape[2]), 0, 1)


def kernel(x_prompt, x_sample, cache_kv_cmp, cache_kv_slc, cache_kv_win, state_rnn_h, state_rnn_conv, state_ffn_conv, page_table, norm_mix, w_in, rnn_conv_w, rnn_conv_b, rnn_wa, rnn_ba, rnn_wx, rnn_bx, rnn_lambda, cmp_w, cmp_pos, w_proj_rnn, w_proj_att, w_out, norm_ffn, ffn_w_gate, ffn_w_up, ffn_conv_w, ffn_conv_b, ffn_w_down, norm_final):
    B, T, _ = x_prompt.shape
    DB, Tq, _ = x_sample.shape
    depth = w_in.shape[0]
    n_pool, page = cache_kv_cmp.shape[1:3]
    n_pages = page_table.shape[1]
    past = n_pages * page
    wbuf = cache_kv_win.shape[2]
    tb = SUBLANES
    assert page % CMP_STRIDE == 0 and page % SLC_BLOCK == 0 and Tq >= RNN_CONV - 1 and Tq <= tb
    assert Tq & (Tq - 1) == 0 and DB % SUBLANES == 0 and T % 256 == 0
    kv_shape = (2, N_KV, HEAD_DIM)

    xp = x_prompt.reshape(B * T, D_MODEL)
    xs = _time_major(x_sample)[0]
    pst = [[] for _ in range(6)]
    sst = [[] for _ in range(6)]
    g_out = norm_final.reshape(1, D_MODEL)
    for l in range(depth):
        w = w_in[l]
        widths = (D_RNN, D_RNN, D_ATT, KV_W, KV_W, KV_W, 3 * N_HEADS, 2 * D_MODEL)
        cuts = [0]
        for wd_ in widths:
            cuts.append(cuts[-1] + wd_)
        part = lambda k: w[:, cuts[k]:cuts[k + 1]]
        w_cat = jnp.concatenate(
            [part(0), part(1), part(7), part(2), part(3), part(4), part(5),
             jnp.pad(part(6), ((0, 0), (0, GN_PAD - 3 * N_HEADS)))], axis=1).astype(bf16)
        g_mix = norm_mix[l].reshape(1, D_MODEL)
        cw, cb = rnn_conv_w[l], rnn_conv_b[l].reshape(1, D_RNN)
        wgate = jnp.concatenate([rnn_wa[l], rnn_wx[l]], axis=-1).astype(bf16)
        ba, bx, lam = (v[l].reshape(1, D_RNN) for v in (rnn_ba, rnn_bx, rnn_lambda))
        cmpw = cmp_w[l].astype(bf16)
        wc = jnp.concatenate([cmpw[:, :CMP_STRIDE].reshape(2, CMP_STRIDE * HEAD_DIM, HEAD_DIM),
                              cmpw[:, CMP_STRIDE:].reshape(2, CMP_STRIDE * HEAD_DIM, HEAD_DIM)], axis=-1)
        pw = cmpw.reshape(2, CMP_BLOCK * HEAD_DIM, HEAD_DIM)
        pos = cmp_pos[l].reshape(2, 1, CMP_BLOCK * HEAD_DIM)
        wr, wa, wo = w_proj_rnn[l].astype(bf16), w_proj_att[l].astype(bf16), w_out[l].astype(bf16)
        g_ffn = norm_ffn[l].reshape(1, D_MODEL)
        fwg, fwu, fwd = ffn_w_gate[l].astype(bf16), ffn_w_up[l].astype(bf16), ffn_w_down[l].astype(bf16)
        fcw, fcb = ffn_conv_w[l], ffn_conv_b[l].reshape(1, D_FF)
        last = l == depth - 1

        per_tok = 2 * N_KV
        z, kvc_rows, kvs_rows, kvw_rows = _inproj(xp, g_mix, w_cat)
        z3 = z.reshape(B, T, D_Z)
        nbr = SUBLANES
        k1r = RNN_CONV - 1
        y_rnn, h_last, rconv = _rglru(z3, jnp.zeros((B * k1r, D_RNN), f32), jnp.zeros((B, D_RNN), f32),
                                      cw, cb, wgate, ba, bx, lam, NB=nbr, S=1, Tt=128)
        rconv = jnp.swapaxes(rconv.reshape(B // nbr, k1r, nbr, D_RNN), 1, 2).reshape(B, k1r, D_RNN)
        kc = _kc_call(kvc_rows.reshape(B, T * per_tok, HEAD_DIM), 1, B, wc, pw, pos)
        o = None
        for sb in range(T // SLC_CHUNK):
            o = _attn_prompt(z3, kc, sb, o)
        x1 = _merge(xp, y_rnn.reshape(B * T, D_RNN), o.reshape(B * T, D_ATT), z, wr, wa, wo)
        y, fconv = _ffn(x1.reshape(B, T, D_MODEL), jnp.zeros((B, FFN_CONV - 1, D_FF), f32), g_ffn, fwg, fwu, fwd,
                        fcw, fcb, g_out, S=1, Tt=512, final_norm=last)
        xp = y.reshape(B * T, D_MODEL)
        wlen = min(WINDOW, T)
        for lst, v in zip(pst, (kvc_rows.reshape(B, T, *kv_shape), kvs_rows.reshape(B, T, *kv_shape),
                                kvw_rows.reshape(B, T, *kv_shape)[:, T - wlen:], h_last, rconv, fconv[:, -1])):
            lst.append(v)

        zs = _inproj(xs, g_mix, w_cat)[0]
        zs3 = zs.reshape(1, Tq * DB, D_Z)
        ys_rnn, hs_last, sconv = _rglru(zs3, _time_major(state_rnn_conv[l])[0], state_rnn_h[l],
                                        cw, cb, wgate, ba, bx, lam, NB=1, S=DB, Tt=Tq)
        zq = _batch_major(zs3[:, :, COL_Q:], DB)
        zq = jnp.pad(zq, ((0, 0), (0, tb - Tq), (0, 0)))
        kcs = _kc_call(cache_kv_cmp[l].reshape(n_pool, page * per_tok, HEAD_DIM), n_pages, DB, wc, pw, pos,
                       page_table=page_table)
        os_, s_win = _attn_sample(zq, kcs, cache_kv_win[l].reshape(DB, wbuf * per_tok, HEAD_DIM),
                                  cache_kv_slc[l].reshape(n_pool, page * per_tok, HEAD_DIM), page_table, tq=Tq)
        os_tm = _time_major(os_[:, :Tq])[0]
        xs1 = _merge(xs, ys_rnn[0], os_tm, zs, wr, wa, wo)
        ys, sfconv = _ffn(xs1[None], _time_major(state_ffn_conv[l]), g_ffn, fwg, fwu, fwd, fcw, fcb, g_out,
                          S=DB, Tt=Tq, final_norm=last)
        xs = ys[0]
        c0 = D_ATT
        for lst, v in zip(sst, (zq[:, :Tq, c0:c0 + KV_W].reshape(DB, Tq, *kv_shape),
                                zq[:, :Tq, c0 + KV_W:c0 + 2 * KV_W].reshape(DB, Tq, *kv_shape),
                                s_win.reshape(DB, wbuf, *kv_shape), hs_last,
                                _batch_major(sconv[None], DB), _batch_major(sfconv[:, -1], DB))):
            lst.append(v)

    y_prompt = xp.reshape(B, T, D_MODEL)
    y_sample = _batch_major(xs[None], DB)
    return (y_prompt, y_sample, *[jnp.stack(v) for v in pst], *[jnp.stack(v) for v in sst])
```
